```python
import math
import jax, jax.numpy as jnp
from jax import lax
import numpy as np

D_MODEL = 1024
BATCH = 2
SEQ = 8192
DEPTH = 4

N_EVEN = (DEPTH + 1) // 2
N_ODD = DEPTH // 2
D_FF = 256 * ((8 * D_MODEL // 3 + 255) // 256)
NORM_EPS = 1e-6
SC_WIDTH = D_MODEL // 2
SC_KERNEL = 3
ATT_HEAD_DIM = 64
ATT_HEADS = D_MODEL // 128
ATT_WIDTH = ATT_HEADS * ATT_HEAD_DIM
IDX_HEADS = D_MODEL // 128
IDX_DIM = ATT_HEAD_DIM
TOPK_MAX = 256
Q_BLOCK = 128
EV_MIX_WIDTH = SC_WIDTH + ATT_WIDTH
EV_PROJ = 3 * SC_WIDTH + 3 * ATT_WIDTH + IDX_HEADS * IDX_DIM + IDX_DIM + IDX_HEADS
DN_HEAD_DIM = 128
DN_HEADS = D_MODEL // 128
DN_WIDTH = DN_HEADS * DN_HEAD_DIM
DN_KERNEL = 4
CHUNK = 64
OD_PROJ = 4 * DN_WIDTH + 2 * DN_HEADS
ROPE_THETA = 500000.0
ROPE_DIM = ATT_HEAD_DIM // 4

kernel_name = "hybrid_conv_dsa_gdn_macaron"


def rmsnorm(x, gain):
    xf = x.astype(jnp.float32)
    y = xf * lax.rsqrt(jnp.mean(xf * xf, axis=-1, keepdims=True) + NORM_EPS)
    return (y * gain.astype(jnp.float32)).astype(x.dtype)


def l2norm(x):
    xf = x.astype(jnp.float32)
    return (xf * lax.rsqrt(jnp.sum(xf * xf, axis=-1, keepdims=True) + NORM_EPS)).astype(x.dtype)


def split_cols(t, sizes):
    return jnp.split(t, np.cumsum(sizes)[:-1].tolist(), axis=-1)


def swiglu(h, w_gate, w_up, w_down):
    return (jax.nn.silu(h @ w_gate) * (h @ w_up)) @ w_down


def rope_tables(positions):
    inv_freq = ROPE_THETA ** (-jnp.arange(0, ROPE_DIM, 2, dtype=jnp.float32) / ROPE_DIM)
    ang = positions.astype(jnp.float32)[..., None] * inv_freq
    return jnp.cos(ang), jnp.sin(ang)


def apply_partial_rope(x, cos, sin):
    half = ROPE_DIM // 2
    c = cos[:, :, None, :].astype(x.dtype)
    s = sin[:, :, None, :].astype(x.dtype)
    x1, x2, rest = x[..., :half], x[..., half:ROPE_DIM], x[..., ROPE_DIM:]
    return jnp.concatenate([x1 * c - x2 * s, x2 * c + x1 * s, rest], axis=-1)


def causal_depthwise_conv(u, w):
    width, seq = w.shape[0], u.shape[1]
    up = jnp.pad(u, ((0, 0), (width - 1, 0), (0, 0)))
    out = up[:, 0:seq] * w[0]
    for j in range(1, width):
        out = out + up[:, j:j + seq] * w[j]
    return out


def dsa_attention(q, k, v, q_idx, k_idx, w_idx, top_k):
    bsz, seq, heads, hd = q.shape
    n_blocks = seq // Q_BLOCK
    key_pos = jnp.arange(seq, dtype=jnp.int32)
    scale = hd ** -0.5
    gather = jax.vmap(lambda t, i: t[i])

    def blocks(t):
        return jnp.moveaxis(t.reshape((bsz, n_blocks, Q_BLOCK) + t.shape[2:]), 1, 0)

    def one_block(args):
        start, qb, qib, wb = args
        tpos = start + jnp.arange(Q_BLOCK, dtype=jnp.int32)
        admissible = key_pos[None, :] <= tpos[:, None]
        score = jax.nn.relu(jnp.einsum('bthd,bsd->bths', qib, k_idx))
        score = jnp.einsum('bths,bth->bts', score, wb).astype(jnp.float32)
        score = jnp.where(admissible[None], score, -jnp.inf)
        _, sel = lax.top_k(score, top_k)
        k_sel = gather(k, sel)
        v_sel = gather(v, sel)
        logits = jnp.einsum('bthd,btkhd->bthk', qb, k_sel).astype(jnp.float32) * scale
        valid = (sel <= tpos[None, :, None])[:, :, None, :]
        probs = jax.nn.softmax(jnp.where(valid, logits, -jnp.inf), axis=-1).astype(v.dtype)
        return jnp.einsum('bthk,btkhd->bthd', probs, v_sel)

    starts = jnp.arange(n_blocks, dtype=jnp.int32) * Q_BLOCK
    out = lax.map(one_block, (starts, blocks(q), blocks(q_idx), blocks(w_idx)))
    return jnp.moveaxis(out, 0, 1).reshape(bsz, seq, heads * hd)


def conv_sparse_attn_mixer(h, cos, sin, w_in, conv_w, w_out, top_k):
    bsz, seq, _ = h.shape
    parts = split_cols(h @ w_in, [SC_WIDTH] * 3 + [ATT_WIDTH] * 3 + [IDX_HEADS * IDX_DIM, IDX_DIM, IDX_HEADS])
    sc_b, sc_c, sc_x, q, k, v, q_idx, k_idx, w_idx = parts
    y_conv = sc_b * causal_depthwise_conv(sc_c * sc_x, conv_w)
    q = apply_partial_rope(q.reshape(bsz, seq, ATT_HEADS, ATT_HEAD_DIM), cos, sin)
    k = apply_partial_rope(k.reshape(bsz, seq, ATT_HEADS, ATT_HEAD_DIM), cos, sin)
    v = v.reshape(bsz, seq, ATT_HEADS, ATT_HEAD_DIM)
    q_idx = apply_partial_rope(q_idx.reshape(bsz, seq, IDX_HEADS, IDX_DIM), cos, sin)
    k_idx = apply_partial_rope(k_idx[:, :, None, :], cos, sin)[:, :, 0]
    w_idx = w_idx * (IDX_HEADS * IDX_DIM) ** -0.5
    y_attn = dsa_attention(q, k, v, q_idx, k_idx, w_idx, top_k)
    return jnp.concatenate([y_conv, y_attn], axis=-1) @ w_out


def gated_delta_rule(q, k, v, g, beta):
    bsz, seq, heads, dk = q.shape
    dv = v.shape[-1]
    n_chunks = seq // CHUNK

    def chunks(t):
        t = jnp.moveaxis(t, 2, 1).astype(jnp.float32)
        return t.reshape(t.shape[:2] + (n_chunks, CHUNK) + t.shape[3:])

    q, k, v, g, beta = (chunks(t) for t in (q, k, v, g, beta))
    g = jnp.cumsum(g, axis=-1)
    causal = jnp.tril(jnp.ones((CHUNK, CHUNK), dtype=bool))
    strict = jnp.tril(jnp.ones((CHUNK, CHUNK), dtype=bool), k=-1)
    diff = g[..., :, None] - g[..., None, :]
    decay = jnp.where(causal, jnp.exp(jnp.where(causal, diff, 0.0)), 0.0)
    k_beta = k * beta[..., None]
    a_mat = jnp.where(strict, jnp.einsum('bhnid,bhnjd->bhnij', k_beta, k) * decay, 0.0)
    eye = jnp.eye(CHUNK, dtype=jnp.float32)
    t_inv = lax.linalg.triangular_solve(a_mat + eye, jnp.broadcast_to(eye, a_mat.shape),
                                        left_side=True, lower=True, unit_diagonal=True)
    u = jnp.einsum('bhnij,bhnjd->bhnid', t_inv, v * beta[..., None])
    w = jnp.einsum('bhnij,bhnjd->bhnid', t_inv, k_beta * jnp.exp(g)[..., None])
    intra = jnp.einsum('bhnid,bhnjd->bhnij', q, k) * decay
    q_dec = q * jnp.exp(g)[..., None]
    k_dec = k * jnp.exp(g[..., -1:] - g)[..., None]
    chunk_decay = jnp.exp(g[..., -1])
    xs = tuple(jnp.moveaxis(t, 2, 0) for t in (u, w, intra, q_dec, k_dec, chunk_decay))

    def step(state, inp):
        u_n, w_n, intra_n, q_n, k_n, d_n = inp
        v_new = u_n - jnp.einsum('bhcd,bhde->bhce', w_n, state)
        out = jnp.einsum('bhcd,bhde->bhce', q_n, state) + jnp.einsum('bhij,bhje->bhie', intra_n, v_new)
        state = state * d_n[..., None, None] + jnp.einsum('bhcd,bhce->bhde', k_n, v_new)
        return state, out

    state0 = jnp.zeros((bsz, heads, dk, dv), jnp.float32)
    _, out = lax.scan(step, state0, xs)
    out = jnp.moveaxis(out, 0, 2).reshape(bsz, heads, seq, dv)
    return jnp.moveaxis(out, 1, 2)


def gated_deltanet_mixer(h, w_in, conv_w, a_log, dt_bias, o_gain, w_out):
    bsz, seq, _ = h.shape
    qkv, z, b_raw, a_raw = split_cols(h @ w_in, [3 * DN_WIDTH, DN_WIDTH, DN_HEADS, DN_HEADS])
    qkv = jax.nn.silu(causal_depthwise_conv(qkv, conv_w))
    q, k, v = (t.reshape(bsz, seq, DN_HEADS, DN_HEAD_DIM) for t in jnp.split(qkv, 3, axis=-1))
    q = l2norm(q) * DN_HEAD_DIM ** -0.5
    k = l2norm(k)
    beta = jax.nn.sigmoid(b_raw.astype(jnp.float32))
    g = -jnp.exp(a_log.astype(jnp.float32)) * jax.nn.softplus(a_raw.astype(jnp.float32) + dt_bias.astype(jnp.float32))
    o = gated_delta_rule(q, k, v, g, beta).astype(h.dtype)
    o = rmsnorm(o, o_gain) * jax.nn.silu(z.reshape(bsz, seq, DN_HEADS, DN_HEAD_DIM))
    return o.reshape(bsz, seq, DN_WIDTH) @ w_out


def setup_inputs(seed: int = 0) -> dict:
    key = jax.random.key(seed)
    ks = jax.random.split(key, 16)
    f32 = jnp.float32

    def nrm(k, shape, fan_in):
        return jax.random.normal(k, shape, f32) * fan_in ** -0.5

    x = jax.random.normal(ks[0], (BATCH, SEQ, D_MODEL), f32)
    positions = jnp.broadcast_to(jnp.arange(SEQ, dtype=jnp.int32), (BATCH, SEQ))
    norm_gain = 1.0 + 0.02 * jax.random.normal(ks[1], (DEPTH, 3, D_MODEL), f32)
    ffn_w_gate = nrm(ks[2], (DEPTH, 2, D_MODEL, D_FF), D_MODEL)
    ffn_w_up = nrm(ks[3], (DEPTH, 2, D_MODEL, D_FF), D_MODEL)
    ffn_w_down = nrm(ks[4], (DEPTH, 2, D_FF, D_MODEL), D_FF)
    ev_w_in = nrm(ks[5], (N_EVEN, D_MODEL, EV_PROJ), D_MODEL)
    ev_conv_w = nrm(ks[6], (N_EVEN, SC_KERNEL, SC_WIDTH), SC_KERNEL)
    ev_w_out = nrm(ks[7], (N_EVEN, EV_MIX_WIDTH, D_MODEL), EV_MIX_WIDTH)
    od_w_in = nrm(ks[8], (N_ODD, D_MODEL, OD_PROJ), D_MODEL)
    od_conv_w = nrm(ks[9], (N_ODD, DN_KERNEL, 3 * DN_WIDTH), DN_KERNEL)
    od_a_log = jnp.log(jax.random.uniform(ks[10], (N_ODD, DN_HEADS), f32, 1.0, 16.0))
    dt = jnp.exp(jax.random.uniform(ks[11], (N_ODD, DN_HEADS), f32, math.log(1e-3), math.log(1e-1)))
    od_dt_bias = dt + jnp.log(-jnp.expm1(-dt))
    od_o_gain = 1.0 + 0.02 * jax.random.normal(ks[12], (N_ODD, DN_HEAD_DIM), f32)
    od_w_out = nrm(ks[13], (N_ODD, DN_WIDTH, D_MODEL), DN_WIDTH)
    final_gain = 1.0 + 0.02 * jax.random.normal(ks[14], (D_MODEL,), f32)
    return {"x": x, "positions": positions, "norm_gain": norm_gain,
            "ffn_w_gate": ffn_w_gate, "ffn_w_up": ffn_w_up, "ffn_w_down": ffn_w_down,
            "ev_w_in": ev_w_in, "ev_conv_w": ev_conv_w, "ev_w_out": ev_w_out,
            "od_w_in": od_w_in, "od_conv_w": od_conv_w, "od_a_log": od_a_log,
            "od_dt_bias": od_dt_bias, "od_o_gain": od_o_gain, "od_w_out": od_w_out,
            "final_gain": final_gain}


def reference(x, positions, norm_gain, ffn_w_gate, ffn_w_up, ffn_w_down, ev_w_in, ev_conv_w, ev_w_out,
              od_w_in, od_conv_w, od_a_log, od_dt_bias, od_o_gain, od_w_out, final_gain):
    top_k = min(TOPK_MAX, x.shape[1] // 4)
    cos, sin = rope_tables(positions)
    for layer in range(DEPTH):
        i = layer // 2
        x = x + 0.5 * swiglu(rmsnorm(x, norm_gain[layer, 0]), ffn_w_gate[layer, 0], ffn_w_up[layer, 0], ffn_w_down[layer, 0])
        h = rmsnorm(x, norm_gain[layer, 1])
        if layer % 2 == 0:
            x = x + conv_sparse_attn_mixer(h, cos, sin, ev_w_in[i], ev_conv_w[i], ev_w_out[i], top_k)
        else:
            x = x + gated_deltanet_mixer(h, od_w_in[i], od_conv_w[i], od_a_log[i], od_dt_bias[i], od_o_gain[i], od_w_out[i])
        x = x + 0.5 * swiglu(rmsnorm(x, norm_gain[layer, 2]), ffn_w_gate[layer, 1], ffn_w_up[layer, 1], ffn_w_down[layer, 1])
    return rmsnorm(x, final_gain)
```

```python
import functools
import math

import jax
import jax.numpy as jnp
from jax import lax
from jax.experimental import pallas as pl
from jax.experimental.pallas import tpu as pltpu

F32 = jnp.float32
BF16 = jnp.bfloat16
I32 = jnp.int32

NORM_EPS = 1e-6
ATT_HEAD_DIM = 64
ATT_SCALE = ATT_HEAD_DIM ** -0.5
assert ATT_SCALE == 2.0 ** round(math.log2(ATT_SCALE)), "folding the scale into bf16 q needs a power of two"
IDX_DIM = 64
DN_HEAD_DIM = 128
DN_CHUNK = 64
DN_KERNEL_ROWS = 8
TOPK_MAX = 256
ROPE_DIM = 16
ROPE_THETA = 500000.0

V7X_LANES = 128
V7X_VMEM_BYTES = 64 * 2**20
INT_MIN = -(2**31)
NEG_BIG = -1e30


def _cparams(semantics, vmem_mb):
    assert vmem_mb * 2**20 < V7X_VMEM_BYTES
    return pltpu.CompilerParams(dimension_semantics=semantics, vmem_limit_bytes=vmem_mb * 2**20)


def _pick_tile(n, target, mult):
    best = None
    t = mult
    while t <= min(n, target):
        if n % t == 0:
            best = t
        t += mult
    assert best is not None, (n, target, mult)
    return best


def _rmsnorm(x, gain):
    ms = jnp.mean(x * x, axis=-1, keepdims=True)
    return x * lax.rsqrt(ms + NORM_EPS) * gain


def _silu(x):
    return x * jax.nn.sigmoid(x)


def _dot_nt(a, b):
    return lax.dot_general(a, b, (((1,), (1,)), ((), ())), preferred_element_type=F32)


def _ffn_body(x_ref, g_ref, wg_ref, wu_ref, wd_ref, *rest, final):
    if final:
        fg_ref, o_ref, h_ref, acc_ref = rest
    else:
        o_ref, h_ref, acc_ref = rest
    f = pl.program_id(1)

    @pl.when(f == 0)
    def _():
        h_ref[...] = _rmsnorm(x_ref[...], g_ref[...]).astype(BF16)
        acc_ref[...] = jnp.zeros_like(acc_ref)

    h = h_ref[...]
    a = jnp.dot(h, wg_ref[...], preferred_element_type=F32)
    b = jnp.dot(h, wu_ref[...], preferred_element_type=F32)
    t = (_silu(a) * b).astype(BF16)
    acc_ref[...] += jnp.dot(t, wd_ref[...], preferred_element_type=F32)

    @pl.when(f == pl.num_programs(1) - 1)
    def _():
        y = x_ref[...] + 0.5 * acc_ref[...]
        if final:
            y = _rmsnorm(y, fg_ref[...])
        o_ref[...] = y


def _ffn(x, gain, wg, wu, wd, widx, final_gain=None):
    n, d = x.shape
    dff = wg.shape[-1]
    tm = _pick_tile(n, 1024, 8)
    tf = _pick_tile(dff, 512, V7X_LANES)
    final = final_gain is not None
    in_specs = [
        pl.BlockSpec((tm, d), lambda i, f: (i, 0)),
        pl.BlockSpec((1, d), lambda i, f: (0, 0)),
        pl.BlockSpec((None, d, tf), lambda i, f: (widx, 0, f)),
        pl.BlockSpec((None, d, tf), lambda i, f: (widx, 0, f)),
        pl.BlockSpec((None, tf, d), lambda i, f: (widx, f, 0)),
    ]
    args = [x, gain.reshape(1, d), wg, wu, wd]
    if final:
        in_specs.append(pl.BlockSpec((1, d), lambda i, f: (0, 0)))
        args.append(final_gain.reshape(1, d))
    return pl.pallas_call(
        functools.partial(_ffn_body, final=final),
        grid=(n // tm, dff // tf),
        in_specs=in_specs,
        out_specs=pl.BlockSpec((tm, d), lambda i, f: (i, 0)),
        out_shape=jax.ShapeDtypeStruct((n, d), F32),
        scratch_shapes=[pltpu.VMEM((tm, d), BF16), pltpu.VMEM((tm, d), F32)],
        compiler_params=_cparams(("parallel", "arbitrary"), 48),
        name="ffn_final" if final else "ffn",
    )(*args)


def _project_rows(x_ref, g_ref, w_ref, p_ref, tiles_per_seq):
    i = pl.program_id(0)
    tm = x_ref.shape[0]
    hist = DN_KERNEL_ROWS

    @pl.when(i % tiles_per_seq == 0)
    def _():
        p_ref[0:hist, :] = jnp.zeros((hist, p_ref.shape[1]), F32)

    @pl.when(i % tiles_per_seq != 0)
    def _():
        p_ref[0:hist, :] = p_ref[tm:tm + hist, :]

    h = _rmsnorm(x_ref[...], g_ref[...]).astype(BF16)
    p_ref[hist:hist + tm, :] = jnp.dot(h, w_ref[...], preferred_element_type=F32)


def _rope(x, c, sa, sb):
    n = x.shape[1]
    reps = n // V7X_LANES
    if reps > 1:
        c, sa, sb = (jnp.concatenate([t] * reps, axis=1) for t in (c, sa, sb))
    half = ROPE_DIM // 2
    x_up = pltpu.roll(x, n - half, 1)
    x_dn = pltpu.roll(x, half, 1)
    return x * c + x_up * sa + x_dn * sb


def _ev_prep_body(x_ref, g_ref, w_ref, cw_ref, c_ref, sa_ref, sb_ref,
                  yc_ref, q_ref, k_ref, v_ref, qi_ref, kiw_ref, p_ref, *, tiles_per_seq, width, w_scale):
    tm = x_ref.shape[0]
    hist = DN_KERNEL_ROWS
    _project_rows(x_ref, g_ref, w_ref, p_ref, tiles_per_seq)
    wd = width

    def rows(shift, col):
        return p_ref[hist - shift:hist - shift + tm, col * wd:(col + 1) * wd]

    ktaps = cw_ref.shape[0]
    conv = None
    for j in range(ktaps):
        shift = ktaps - 1 - j
        term = (rows(shift, 1) * rows(shift, 2)) * cw_ref[j:j + 1, :]
        conv = term if conv is None else conv + term
    yc_ref[...] = (rows(0, 0) * conv).astype(BF16)

    c, sa, sb = c_ref[...], sa_ref[...], sb_ref[...]
    q_ref[...] = (_rope(rows(0, 3), c, sa, sb) * ATT_SCALE).astype(BF16)
    k_ref[...] = _rope(rows(0, 4), c, sa, sb).astype(BF16)
    v_ref[...] = rows(0, 5).astype(BF16)
    qi_ref[...] = _rope(rows(0, 6), c, sa, sb).astype(BF16)
    slab = p_ref[hist:hist + tm, 7 * wd:7 * wd + V7X_LANES]
    lane = lax.broadcasted_iota(I32, slab.shape, 1)
    kiw_ref[...] = jnp.where(lane < IDX_DIM, _rope(slab, c, sa, sb), slab * w_scale)


def _ev_prep(x, gain, w_in, conv_w, tables, seq):
    n, d = x.shape
    wd = d // 2
    ppad = w_in.shape[1]
    tm = _pick_tile(seq, 256, 8)
    row = lambda i: (i, 0)
    fixed = lambda i: (0, 0)
    n_idx_heads = wd // IDX_DIM
    outs = [jax.ShapeDtypeStruct((n, wd), BF16)] * 5 + [jax.ShapeDtypeStruct((n, V7X_LANES), F32)]
    return pl.pallas_call(
        functools.partial(_ev_prep_body, tiles_per_seq=seq // tm, width=wd,
                          w_scale=float(n_idx_heads * IDX_DIM) ** -0.5),
        grid=(n // tm,),
        in_specs=[pl.BlockSpec((tm, d), row), pl.BlockSpec((1, d), fixed), pl.BlockSpec((d, ppad), fixed),
                  pl.BlockSpec(conv_w.shape, fixed)] + [pl.BlockSpec((tm, V7X_LANES), row)] * 3,
        out_specs=[pl.BlockSpec((tm, wd), row)] * 5 + [pl.BlockSpec((tm, V7X_LANES), row)],
        out_shape=outs,
        scratch_shapes=[pltpu.VMEM((tm + DN_KERNEL_ROWS, ppad), F32)],
        compiler_params=_cparams(("arbitrary",), 48),
        name="ev_prep",
    )(x, gain.reshape(1, d), w_in, conv_w, *tables)


KEY_NEG_FLT_MAX = INT_MIN + (1 << 23)
COUNT_ROWS = 64


def _key_to_float(key):
    return pltpu.bitcast(key ^ ((key >> 31) & 0x7FFFFFFF), F32)


def _topk_mask_body(qi_ref, wt_ref, ki_ref, mask_ref, sc_ref, *, tq, tk, n_heads, top_k, seq):
    i = pl.program_id(1)
    n_kt = seq // tk
    q0 = i * tq
    n_act = (q0 + tq + tk - 1) // tk
    q_pos = q0 + lax.broadcasted_iota(I32, (1, tq), 1)
    key_iota = lax.broadcasted_iota(I32, (tk, tq), 0)
    wt = wt_ref[0]

    q_pairs = [jnp.concatenate([qi_ref[0, :, (2 * p) * IDX_DIM:(2 * p + 1) * IDX_DIM],
                                qi_ref[0, :, (2 * p + 1) * IDX_DIM:(2 * p + 2) * IDX_DIM]], axis=0)
               for p in range(n_heads // 2)]
    sub = V7X_LANES

    def score_tile(j, carry):
        for rb in range(tk // sub):
            kr = ki_ref[0, j, rb * sub:(rb + 1) * sub, :]
            acc = jnp.zeros((sub, tq), F32)
            for p in range(n_heads // 2):
                s2 = _dot_nt(kr, q_pairs[p])
                acc = (acc + jnp.maximum(s2[:, :tq], 0.0) * wt[2 * p:2 * p + 1, :]
                       + jnp.maximum(s2[:, tq:], 0.0) * wt[2 * p + 1:2 * p + 2, :])
            key_pos = j * tk + rb * sub + key_iota[:sub]
            sc_ref[j, rb * sub:(rb + 1) * sub, :] = jnp.where(key_pos <= q_pos, acc, -jnp.inf)
        return carry

    lax.fori_loop(0, n_act, score_tile, 0)

    def count(pred):
        def body(j, acc):
            hit = jnp.where(pred(sc_ref[j], j), 1.0, 0.0)
            return acc + hit.reshape(tk // COUNT_ROWS, COUNT_ROWS, tq).sum(axis=0)
        acc = lax.fori_loop(0, n_act, body, jnp.zeros((COUNT_ROWS, tq), F32))
        return jnp.sum(acc, axis=0, keepdims=True)

    kf = float(top_k)
    n0 = count(lambda s, j: s >= 0.0)
    thr_key = jnp.where(n0 >= kf, 0, INT_MIN).astype(I32)
    n_ge = jnp.where(n0 >= kf, n0, (q_pos + 1).astype(F32))

    def try_bit(bit, key, n_keep):
        cand = key | jnp.left_shift(jnp.int32(1), bit)
        cf = _key_to_float(cand)
        n = count(lambda s, j: s >= cf)
        ok = n >= kf
        return jnp.where(ok, cand, key), jnp.where(ok, n, n_keep)

    group = 4
    top = 30
    while (top + 1) % group:
        thr_key, n_ge = try_bit(top, thr_key, n_ge)
        top -= 1

    def more(c):
        return jnp.logical_and(c[0] >= 0, jnp.max(c[2]) > kf)

    def bit_group(c):
        hi, key, n_keep = c
        for d in range(group):
            key, n_keep = try_bit(hi - d, key, n_keep)
        return hi - group, key, n_keep

    _, thr_key, n_ge = lax.while_loop(more, bit_group, (jnp.int32(top), thr_key, n_ge))
    thr = _key_to_float(jnp.maximum(thr_key, KEY_NEG_FLT_MAX))
    ambiguous = jnp.max(n_ge) > kf
    n_bits = max(1, (seq - 1).bit_length())

    def tie_cut():
        need = kf - count(lambda s, j: s > thr)

        def step(b, x):
            cand = x | jnp.left_shift(jnp.int32(1), n_bits - 1 - b)
            cnt = count(lambda s, j: jnp.where(s == thr, j * tk + key_iota, seq) < cand)
            return jnp.where(cnt < need, cand, x)
        return lax.fori_loop(0, n_bits, step, jnp.zeros((1, tq), I32))

    cut = lax.cond(ambiguous, tie_cut, lambda: jnp.full((1, tq), seq, I32))

    def write_mask(j, carry):
        s = sc_ref[j]
        tie_pos = jnp.where(s == thr, j * tk + key_iota, seq + 1)
        sel = jnp.where(s > thr, 1, jnp.where(tie_pos <= cut, 1, 0))
        mask_ref[0, j] = sel.astype(mask_ref.dtype)
        return carry

    def write_zero(j, carry):
        mask_ref[0, j] = jnp.zeros((tk, tq), mask_ref.dtype)
        return carry

    lax.fori_loop(0, n_act, write_mask, 0)
    lax.fori_loop(n_act, n_kt, write_zero, 0)


def _topk_mask(qi, wt, ki, top_k, tk):
    b, s, iw = qi.shape
    n_heads = iw // IDX_DIM
    tq = V7X_LANES
    n_kt = s // tk
    return pl.pallas_call(
        functools.partial(_topk_mask_body, tq=tq, tk=tk, n_heads=n_heads, top_k=top_k, seq=s),
        grid=(b, s // tq),
        in_specs=[pl.BlockSpec((1, tq, iw), lambda bi, i: (bi, i, 0)),
                  pl.BlockSpec((1, n_heads, tq), lambda bi, i: (bi, 0, i)),
                  pl.BlockSpec((1, n_kt, tk, IDX_DIM), lambda bi, i: (bi, 0, 0, 0))],
        out_specs=pl.BlockSpec((1, n_kt, tk, tq), lambda bi, i: (bi, 0, 0, i)),
        out_shape=jax.ShapeDtypeStruct((b, n_kt, tk, s), jnp.int8),
        scratch_shapes=[pltpu.VMEM((n_kt, tk, tq), F32)],
        compiler_params=_cparams(("parallel", "arbitrary"), 48),
        name="topk_mask",
    )(qi, wt, ki)


def _attn_body(i_tab, j_tab, q_ref, k_ref, vt_ref, mask_ref, o_ref, m_ref, l_ref, acc_ref, bias_ref, s_ref, *,
               tq, tk, n_heads):
    t = pl.program_id(1)
    i = i_tab[t]
    j = j_tab[t]
    hd = ATT_HEAD_DIM
    blk = V7X_LANES

    @pl.when(j == 0)
    def _():
        m_ref[...] = jnp.full(m_ref.shape, NEG_BIG, F32)
        l_ref[...] = jnp.zeros_like(l_ref)
        acc_ref[...] = jnp.zeros_like(acc_ref)

    bias_ref[...] = jnp.where(mask_ref[0, 0].astype(F32) != 0.0, 0.0, NEG_BIG)
    lane = lax.broadcasted_iota(I32, (1, 2 * hd), 1)
    half_sel = [jnp.where(lane < hd, 1.0, 0.0).astype(BF16), jnp.where(lane >= hd, 1.0, 0.0).astype(BF16)]

    for qt in range(tq // blk):
        qs = slice(qt * blk, (qt + 1) * blk)
        m_tile = []
        for h in range(n_heads):
            pair = slice((h // 2) * 2 * hd, (h // 2 + 1) * 2 * hd)
            qh = q_ref[0, qs, pair] * half_sel[h % 2]
            s = _dot_nt(k_ref[0, :, pair], qh) + bias_ref[:, qs]
            s_ref[h] = s
            m_tile.append(jnp.max(s, axis=0, keepdims=True))
        for h in range(n_heads):
            hs = slice(h * hd, (h + 1) * hd)
            m = m_ref[h:h + 1, qs]
            m_new = jnp.maximum(m, m_tile[h])
            alpha = jnp.exp(m - m_new)
            p = jnp.exp(s_ref[h] - m_new)
            l_ref[h:h + 1, qs] = alpha * l_ref[h:h + 1, qs] + jnp.sum(p, axis=0, keepdims=True)
            m_ref[h:h + 1, qs] = m_new
            pv = jnp.dot(vt_ref[0, hs, :], p.astype(BF16), preferred_element_type=F32)
            acc_ref[hs, qs] = alpha * acc_ref[hs, qs] + pv

    @pl.when(j == ((i + 1) * tq - 1) // tk)
    def _():
        for h in range(n_heads):
            hs = slice(h * hd, (h + 1) * hd)
            o_ref[0, hs, :] = (acc_ref[hs, :] / l_ref[h:h + 1, :]).astype(o_ref.dtype)


def _masked_attention(q, k, vt, mask, tk):
    b, s, aw = q.shape
    n_heads = aw // ATT_HEAD_DIM
    tq = _pick_tile(s, 256, V7X_LANES)
    pairs = [(i, j) for i in range(s // tq) for j in range(((i + 1) * tq - 1) // tk + 1)]
    i_tab = jnp.asarray([p[0] for p in pairs], I32)
    j_tab = jnp.asarray([p[1] for p in pairs], I32)
    grid_spec = pltpu.PrefetchScalarGridSpec(
        num_scalar_prefetch=2,
        grid=(b, len(pairs)),
        in_specs=[pl.BlockSpec((1, tq, aw), lambda bi, t, it, jt: (bi, it[t], 0)),
                  pl.BlockSpec((1, tk, aw), lambda bi, t, it, jt: (bi, jt[t], 0)),
                  pl.BlockSpec((1, aw, tk), lambda bi, t, it, jt: (bi, 0, jt[t])),
                  pl.BlockSpec((1, 1, tk, tq), lambda bi, t, it, jt: (bi, jt[t], 0, it[t]))],
        out_specs=pl.BlockSpec((1, aw, tq), lambda bi, t, it, jt: (bi, 0, it[t])),
        scratch_shapes=[pltpu.VMEM((n_heads, tq), F32), pltpu.VMEM((n_heads, tq), F32),
                        pltpu.VMEM((aw, tq), F32), pltpu.VMEM((tk, tq), F32),
                        pltpu.VMEM((n_heads, tk, V7X_LANES), F32)],
    )
    return pl.pallas_call(
        functools.partial(_attn_body, tq=tq, tk=tk, n_heads=n_heads),
        grid_spec=grid_spec,
        out_shape=jax.ShapeDtypeStruct((b, aw, s), BF16),
        compiler_params=_cparams(("parallel", "arbitrary"), 32),
        name="masked_attn",
    )(i_tab, j_tab, q, k, vt, mask)


def _outproj_body(x_ref, *refs):
    n_in = (len(refs) - 1) // 2
    acc = x_ref[...]
    for y_ref, w_ref in zip(refs[:n_in], refs[n_in:2 * n_in]):
        acc = acc + jnp.dot(y_ref[...], w_ref[...], preferred_element_type=F32)
    refs[-1][...] = acc


def _outproj(x, ys, ws):
    n, d = x.shape
    tm = _pick_tile(n, 512, 8)
    row = lambda i: (i, 0)
    fixed = lambda i: (0, 0)
    return pl.pallas_call(
        _outproj_body,
        grid=(n // tm,),
        in_specs=[pl.BlockSpec((tm, d), row)] + [pl.BlockSpec((tm, y.shape[1]), row) for y in ys]
                 + [pl.BlockSpec(w.shape, fixed) for w in ws],
        out_specs=pl.BlockSpec((tm, d), row),
        out_shape=jax.ShapeDtypeStruct((n, d), F32),
        compiler_params=_cparams(("parallel",), 32),
        name="outproj",
    )(x, *ys, *ws)


def _od_prep_body(x_ref, g_ref, w_ref, cw_ref, alog_ref, dtb_ref,
                  q_ref, k_ref, v_ref, z_ref, bg_ref, p_ref, *, tiles_per_seq, n_heads):
    tm = x_ref.shape[0]
    hist = DN_KERNEL_ROWS
    d = n_heads * DN_HEAD_DIM
    _project_rows(x_ref, g_ref, w_ref, p_ref, tiles_per_seq)
    ktaps = cw_ref.shape[0]
    for part, out_ref in enumerate((q_ref, k_ref, v_ref)):
        for h in range(n_heads):
            col = part * d + h * DN_HEAD_DIM
            conv = None
            for j in range(ktaps):
                shift = ktaps - 1 - j
                term = (p_ref[hist - shift:hist - shift + tm, col:col + DN_HEAD_DIM]
                        * cw_ref[j:j + 1, col:col + DN_HEAD_DIM])
                conv = term if conv is None else conv + term
            u = _silu(conv)
            if part < 2:
                u = u * lax.rsqrt(jnp.sum(u * u, axis=-1, keepdims=True) + NORM_EPS)
            if part == 0:
                u = u * DN_HEAD_DIM ** -0.5
            out_ref[:, h * DN_HEAD_DIM:(h + 1) * DN_HEAD_DIM] = u
    z_ref[...] = p_ref[hist:hist + tm, 3 * d:4 * d]
    slab = p_ref[hist:hist + tm, 4 * d:4 * d + V7X_LANES]
    lane = lax.broadcasted_iota(I32, slab.shape, 1)
    xs = slab + dtb_ref[...]
    softplus = jnp.maximum(xs, 0.0) + jnp.log1p(jnp.exp(-jnp.abs(xs)))
    bg_ref[...] = jnp.where(lane < n_heads, jax.nn.sigmoid(slab), -jnp.exp(alog_ref[...]) * softplus)


def _od_prep(x, gain, w_in, conv_w, a_log, dt_bias, seq):
    n, d = x.shape
    n_heads = d // DN_HEAD_DIM
    ppad = w_in.shape[1]
    tm = _pick_tile(seq, 256, 8)
    row = lambda i: (i, 0)
    fixed = lambda i: (0, 0)
    pad = lambda t: jnp.pad(t.astype(F32), (n_heads, V7X_LANES - 2 * n_heads)).reshape(1, V7X_LANES)
    outs = [jax.ShapeDtypeStruct((n, d), F32)] * 4 + [jax.ShapeDtypeStruct((n, V7X_LANES), F32)]
    return pl.pallas_call(
        functools.partial(_od_prep_body, tiles_per_seq=seq // tm, n_heads=n_heads),
        grid=(n // tm,),
        in_specs=[pl.BlockSpec((tm, d), row), pl.BlockSpec((1, d), fixed), pl.BlockSpec((d, ppad), fixed),
                  pl.BlockSpec(conv_w.shape, fixed), pl.BlockSpec((1, V7X_LANES), fixed),
                  pl.BlockSpec((1, V7X_LANES), fixed)],
        out_specs=[pl.BlockSpec((tm, d), row)] * 4 + [pl.BlockSpec((tm, V7X_LANES), row)],
        out_shape=outs,
        scratch_shapes=[pltpu.VMEM((tm + DN_KERNEL_ROWS, ppad), F32)],
        compiler_params=_cparams(("arbitrary",), 56),
        name="od_prep",
    )(x, gain.reshape(1, d), w_in, conv_w, pad(a_log), pad(dt_bias))


def _split(a):
    hi = a.astype(BF16)
    return hi, (a - hi.astype(F32)).astype(BF16)


def _dot_split(a, b):
    (ah, al), (bh, bl) = a, b
    d = lambda x, y: jnp.dot(x, y, preferred_element_type=F32)
    return d(ah, bh) + d(al, bh) + d(ah, bl)


def _delta_body(q_ref, k_ref, v_ref, z_ref, bg_ref, bgt_ref, gain_ref, y_ref, state_ref, *, n_heads, n_chunks):
    c_len = DN_CHUNK
    hd = DN_HEAD_DIM
    heads = range(n_heads)

    @pl.when(pl.program_id(1) == 0)
    def _():
        state_ref[...] = jnp.zeros_like(state_ref)

    ri = lax.broadcasted_iota(I32, (c_len, c_len), 0)
    ci = lax.broadcasted_iota(I32, (c_len, c_len), 1)
    lower = ci <= ri
    strict = ci < ri
    eye = jnp.where(ci == ri, 1.0, 0.0)

    def chunk_step(c, carry):
        r0 = pl.multiple_of(c * c_len, c_len)
        rows = pl.ds(r0, c_len)
        bg = bg_ref[0, rows, :]
        bgt = bgt_ref[0, c]
        q = [q_ref[0, rows, h * hd:(h + 1) * hd] for h in heads]
        k = [k_ref[0, rows, h * hd:(h + 1) * hd] for h in heads]
        v = [v_ref[0, rows, h * hd:(h + 1) * hd] for h in heads]
        beta = [bg[:, h:h + 1] for h in heads]
        gc_col = [jnp.sum(jnp.where(lower, bgt[n_heads + h:n_heads + h + 1, :], 0.0), axis=1, keepdims=True)
                  for h in heads]
        gc_row = [jnp.sum(jnp.where(ri <= ci, bg[:, n_heads + h:n_heads + h + 1], 0.0), axis=0, keepdims=True)
                  for h in heads]
        decay = [jnp.where(lower, jnp.exp(jnp.where(lower, gc_col[h] - gc_row[h], 0.0)), 0.0) for h in heads]
        kb = [k[h] * beta[h] for h in heads]
        kbf = [k[h].astype(BF16) for h in heads]
        kq = [_dot_nt(jnp.concatenate([kb[h], q[h]], axis=0).astype(BF16), kbf[h]) for h in heads]
        intra = [(kq[h][c_len:] * decay[h]).astype(BF16) for h in heads]
        pw = [jnp.where(strict, -(kq[h][:c_len] * decay[h]), 0.0) for h in heads]
        t_inv = [eye + pw[h] for h in heads]
        pw_parts = [_split(p) for p in pw]
        span = 1
        while 2 * span < c_len:
            pw = [_dot_split(pw_parts[h], pw_parts[h]) for h in heads]
            pw_parts = [_split(p) for p in pw]
            t_inv = [t_inv[h] + _dot_split(_split(t_inv[h]), pw_parts[h]) for h in heads]
            span *= 2
        e_gc = [jnp.exp(gc_col[h]) for h in heads]
        uw = [jnp.dot(t_inv[h].astype(BF16),
                      jnp.concatenate([v[h] * beta[h], kb[h] * e_gc[h]], axis=1).astype(BF16),
                      preferred_element_type=F32) for h in heads]
        g_last = [gc_col[h][c_len - 1:c_len, :] for h in heads]
        k_dec_t = [(k[h] * jnp.exp(g_last[h] - gc_col[h])).T.astype(BF16) for h in heads]
        wq = [jnp.concatenate([uw[h][:, hd:], q[h] * e_gc[h]], axis=0).astype(BF16) for h in heads]
        state = [state_ref[h] for h in heads]
        ws = [jnp.dot(wq[h], state[h].astype(BF16), preferred_element_type=F32) for h in heads]
        v_new = [(uw[h][:, :hd] - ws[h][:c_len]).astype(BF16) for h in heads]
        out = [ws[h][c_len:] + jnp.dot(intra[h], v_new[h], preferred_element_type=F32) for h in heads]
        for h in heads:
            state_ref[h] = (state[h] * jnp.exp(g_last[h])
                            + jnp.dot(k_dec_t[h], v_new[h], preferred_element_type=F32))
        for h in heads:
            o = out[h] * lax.rsqrt(jnp.mean(out[h] * out[h], axis=-1, keepdims=True) + NORM_EPS) * gain_ref[...]
            zh = z_ref[0, rows, h * hd:(h + 1) * hd]
            y_ref[0, rows, h * hd:(h + 1) * hd] = (o * _silu(zh)).astype(y_ref.dtype)
        return carry

    lax.fori_loop(0, n_chunks, chunk_step, 0)


def _delta_rule(q, k, v, z, bg, bgt, o_gain):
    b, s, d = q.shape
    n_heads = d // DN_HEAD_DIM
    rows = _pick_tile(s, 512, DN_CHUNK)
    n_chunks = rows // DN_CHUNK
    blk = lambda bi, i: (bi, i, 0)
    return pl.pallas_call(
        functools.partial(_delta_body, n_heads=n_heads, n_chunks=n_chunks),
        grid=(b, s // rows),
        in_specs=[pl.BlockSpec((1, rows, d), blk)] * 4 + [
            pl.BlockSpec((1, rows, V7X_LANES), blk),
            pl.BlockSpec((1, n_chunks, 2 * n_heads, DN_CHUNK), lambda bi, i: (bi, i, 0, 0)),
            pl.BlockSpec((1, DN_HEAD_DIM), lambda bi, i: (0, 0))],
        out_specs=pl.BlockSpec((1, rows, d), blk),
        out_shape=jax.ShapeDtypeStruct((b, s, d), BF16),
        scratch_shapes=[pltpu.VMEM((n_heads, DN_HEAD_DIM, DN_HEAD_DIM), F32)],
        compiler_params=_cparams(("parallel", "arbitrary"), 40),
        name="delta_rule",
    )(q, k, v, z, bg, bgt, o_gain.reshape(1, DN_HEAD_DIM))


def _rope_tables(positions):
    half = ROPE_DIM // 2
    inv_freq = ROPE_THETA ** (-jnp.arange(0, ROPE_DIM, 2, dtype=F32) / ROPE_DIM)
    ang = positions.astype(F32).reshape(-1, 1) * inv_freq
    cos, sin = jnp.cos(ang), jnp.sin(ang)
    n = ang.shape[0]
    rest = ATT_HEAD_DIM - ROPE_DIM
    zeros = lambda m: jnp.zeros((n, m), F32)
    c = jnp.concatenate([cos, cos, jnp.ones((n, rest), F32)], axis=1)
    sa = jnp.concatenate([-sin, zeros(half + rest)], axis=1)
    sb = jnp.concatenate([zeros(half), sin, zeros(rest)], axis=1)
    reps = V7X_LANES // ATT_HEAD_DIM
    return tuple(jnp.tile(t, (1, reps)) for t in (c, sa, sb))


def _pad_cols(w, mult):
    pad = (-w.shape[-1]) % mult
    return jnp.pad(w, [(0, 0)] * (w.ndim - 1) + [(0, pad)])


def kernel(x, positions, norm_gain, ffn_w_gate, ffn_w_up, ffn_w_down, ev_w_in, ev_conv_w, ev_w_out,
           od_w_in, od_conv_w, od_a_log, od_dt_bias, od_o_gain, od_w_out, final_gain):
    bsz, seq, d = x.shape
    depth = norm_gain.shape[0]
    n = bsz * seq
    assert d % (2 * V7X_LANES) == 0 and seq % V7X_LANES == 0
    top_k = min(TOPK_MAX, seq // 4)
    tk = _pick_tile(seq, 512, V7X_LANES)
    dff = ffn_w_gate.shape[-1]

    wg = ffn_w_gate.astype(BF16).reshape(depth * 2, d, dff)
    wu = ffn_w_up.astype(BF16).reshape(depth * 2, d, dff)
    wd = ffn_w_down.astype(BF16).reshape(depth * 2, dff, d)
    ev_in = _pad_cols(ev_w_in, V7X_LANES).astype(BF16)
    ev_out = ev_w_out.astype(BF16)
    od_in = _pad_cols(od_w_in, V7X_LANES).astype(BF16)
    od_out = od_w_out.astype(BF16)
    tables = _rope_tables(positions)
    half_w = d // 2

    x = x.reshape(n, d)
    for layer in range(depth):
        i = layer // 2
        x = _ffn(x, norm_gain[layer, 0], wg, wu, wd, 2 * layer)
        if layer % 2 == 0:
            yc, q, k, v, qi, kiw = _ev_prep(x, norm_gain[layer, 1], ev_in[i], ev_conv_w[i], tables, seq)
            kiw3 = kiw.reshape(bsz, seq, V7X_LANES)
            ki = kiw3[:, :, :IDX_DIM].astype(BF16).reshape(bsz, seq // tk, tk, IDX_DIM)
            n_idx = half_w // IDX_DIM
            wt = jnp.swapaxes(kiw3[:, :, IDX_DIM:IDX_DIM + n_idx], 1, 2)
            to3 = lambda t: t.reshape(bsz, seq, half_w)
            mask = _topk_mask(to3(qi), wt, ki, top_k, tk)
            ya_t = _masked_attention(to3(q), to3(k), jnp.swapaxes(to3(v), 1, 2), mask, tk)
            ya = jnp.swapaxes(ya_t, 1, 2).reshape(n, half_w)
            x = _outproj(x, [yc, ya], [ev_out[i, :half_w], ev_out[i, half_w:]])
        else:
            q, k, v, z, bg = _od_prep(x, norm_gain[layer, 1], od_in[i], od_conv_w[i], od_a_log[i],
                                      od_dt_bias[i], seq)
            n_heads = d // DN_HEAD_DIM
            bg3 = bg.reshape(bsz, seq, V7X_LANES)
            bgt = jnp.swapaxes(bg3[:, :, :2 * n_heads].reshape(bsz, seq // DN_CHUNK, DN_CHUNK, 2 * n_heads),
                               2, 3)
            to3 = lambda t: t.reshape(bsz, seq, d)
            y = _delta_rule(to3(q), to3(k), to3(v), to3(z), bg3, bgt, od_o_gain[i]).reshape(n, d)
            x = _outproj(x, [y], [od_out[i]])
        x = _ffn(x, norm_gain[layer, 2], wg, wu, wd, 2 * layer + 1,
                 final_gain=final_gain if layer == depth - 1 else None)
    return x.reshape(bsz, seq, d)
```

```python
import functools
import math

import jax
import jax.numpy as jnp
from jax import lax
from jax.experimental import pallas as pl
from jax.experimental.pallas import tpu as pltpu

F32 = jnp.float32
BF16 = jnp.bfloat16
I32 = jnp.int32

NORM_EPS = 1e-6
ATT_HEAD_DIM = 64
ATT_SCALE = ATT_HEAD_DIM ** -0.5
assert ATT_SCALE == 2.0 ** round(math.log2(ATT_SCALE)), "folding the scale into bf16 q needs a power of two"
IDX_DIM = 64
DN_HEAD_DIM = 128
DN_CHUNK = 64
DN_KERNEL_ROWS = 8
TOPK_MAX = 256
ROPE_DIM = 16
ROPE_THETA = 500000.0

V7X_LANES = 128
BF16_SUBLANES = 16
V7X_VMEM_BYTES = 64 * 2**20
INT_MIN = -(2**31)
NEG_BIG = -1e30


def _cparams(semantics, vmem_mb):
    assert vmem_mb * 2**20 < V7X_VMEM_BYTES
    return pltpu.CompilerParams(dimension_semantics=semantics, vmem_limit_bytes=vmem_mb * 2**20)


def _pick_tile(n, target, mult):
    best = None
    t = mult
    while t <= min(n, target):
        if n % t == 0:
            best = t
        t += mult
    assert best is not None, (n, target, mult)
    return best


def _rmsnorm(x, gain):
    ms = jnp.mean(x * x, axis=-1, keepdims=True)
    return x * lax.rsqrt(ms + NORM_EPS) * gain


def _silu(x):
    return x * jax.nn.sigmoid(x)


def _dot_nt(a, b):
    return lax.dot_general(a, b, (((1,), (1,)), ((), ())), preferred_element_type=F32)


def _ffn_body(x_ref, g_ref, wg_ref, wu_ref, wd_ref, *rest, final):
    if final:
        fg_ref, o_ref, h_ref, acc_ref = rest
    else:
        o_ref, h_ref, acc_ref = rest
    f = pl.program_id(1)

    @pl.when(f == 0)
    def _():
        h_ref[...] = _rmsnorm(x_ref[...], g_ref[...]).astype(BF16)
        acc_ref[...] = jnp.zeros_like(acc_ref)

    h = h_ref[...]
    a = jnp.dot(h, wg_ref[...].astype(BF16), preferred_element_type=F32)
    b = jnp.dot(h, wu_ref[...].astype(BF16), preferred_element_type=F32)
    t = (_silu(a) * b).astype(BF16)
    acc_ref[...] += jnp.dot(t, wd_ref[...].astype(BF16), preferred_element_type=F32)

    @pl.when(f == pl.num_programs(1) - 1)
    def _():
        y = x_ref[...] + 0.5 * acc_ref[...]
        if final:
            y = _rmsnorm(y, fg_ref[...])
        o_ref[...] = y


def _ffn(x, gain, wg, wu, wd, widx, final_gain=None):
    n, d = x.shape
    dff = wg.shape[-1]
    tm = _pick_tile(n, 1024, 8)
    tf = _pick_tile(dff, 512, V7X_LANES)
    final = final_gain is not None
    in_specs = [
        pl.BlockSpec((tm, d), lambda i, f: (i, 0)),
        pl.BlockSpec((1, d), lambda i, f: (0, 0)),
        pl.BlockSpec((None, d, tf), lambda i, f: (widx, 0, f)),
        pl.BlockSpec((None, d, tf), lambda i, f: (widx, 0, f)),
        pl.BlockSpec((None, tf, d), lambda i, f: (widx, f, 0)),
    ]
    args = [x, gain.reshape(1, d), wg, wu, wd]
    if final:
        in_specs.append(pl.BlockSpec((1, d), lambda i, f: (0, 0)))
        args.append(final_gain.reshape(1, d))
    return pl.pallas_call(
        functools.partial(_ffn_body, final=final),
        grid=(n // tm, dff // tf),
        in_specs=in_specs,
        out_specs=pl.BlockSpec((tm, d), lambda i, f: (i, 0)),
        out_shape=jax.ShapeDtypeStruct((n, d), F32),
        scratch_shapes=[pltpu.VMEM((tm, d), BF16), pltpu.VMEM((tm, d), F32)],
        compiler_params=_cparams(("parallel", "arbitrary"), 48),
        name="ffn_final" if final else "ffn",
    )(*args)


def _project_rows(x_ref, g_ref, w_ref, p_ref, tiles_per_seq):
    i = pl.program_id(0)
    tm = x_ref.shape[0]
    hist = DN_KERNEL_ROWS

    @pl.when(i % tiles_per_seq == 0)
    def _():
        p_ref[0:hist, :] = jnp.zeros((hist, p_ref.shape[1]), F32)

    @pl.when(i % tiles_per_seq != 0)
    def _():
        p_ref[0:hist, :] = p_ref[tm:tm + hist, :]

    h = _rmsnorm(x_ref[...], g_ref[...]).astype(BF16)
    p_ref[hist:hist + tm, :] = jnp.dot(h, w_ref[...], preferred_element_type=F32)


def _rope(x, c, sa, sb):
    n = x.shape[1]
    reps = n // V7X_LANES
    if reps > 1:
        c, sa, sb = (jnp.concatenate([t] * reps, axis=1) for t in (c, sa, sb))
    half = ROPE_DIM // 2
    x_up = pltpu.roll(x, n - half, 1)
    x_dn = pltpu.roll(x, half, 1)
    return x * c + x_up * sa + x_dn * sb


def _ev_prep_body(x_ref, g_ref, w_ref, cw_ref, c_ref, sa_ref, sb_ref,
                  yc_ref, q_ref, k_ref, v_ref, qi_ref, kiw_ref, p_ref, *, tiles_per_seq, width, w_scale):
    tm = x_ref.shape[0]
    hist = DN_KERNEL_ROWS
    _project_rows(x_ref, g_ref, w_ref, p_ref, tiles_per_seq)
    wd = width

    def rows(shift, col):
        return p_ref[hist - shift:hist - shift + tm, col * wd:(col + 1) * wd]

    ktaps = cw_ref.shape[0]
    conv = None
    for j in range(ktaps):
        shift = ktaps - 1 - j
        term = (rows(shift, 1) * rows(shift, 2)) * cw_ref[j:j + 1, :]
        conv = term if conv is None else conv + term
    yc_ref[...] = (rows(0, 0) * conv).astype(BF16)

    c, sa, sb = c_ref[...], sa_ref[...], sb_ref[...]
    q_ref[...] = (_rope(rows(0, 3), c, sa, sb) * ATT_SCALE).astype(BF16)
    k_ref[...] = _rope(rows(0, 4), c, sa, sb).astype(BF16)
    v_ref[...] = rows(0, 5).astype(BF16)
    qi_ref[...] = _rope(rows(0, 6), c, sa, sb).astype(BF16)
    slab = p_ref[hist:hist + tm, 7 * wd:7 * wd + V7X_LANES]
    lane = lax.broadcasted_iota(I32, slab.shape, 1)
    kiw_ref[...] = jnp.where(lane < IDX_DIM, _rope(slab, c, sa, sb), slab * w_scale)


def _ev_prep(x, gain, w_in, conv_w, tables, seq):
    n, d = x.shape
    wd = d // 2
    ppad = w_in.shape[1]
    tm = _pick_tile(seq, 256, 8)
    row = lambda i: (i, 0)
    fixed = lambda i: (0, 0)
    n_idx_heads = wd // IDX_DIM
    outs = [jax.ShapeDtypeStruct((n, wd), BF16)] * 5 + [jax.ShapeDtypeStruct((n, V7X_LANES), F32)]
    return pl.pallas_call(
        functools.partial(_ev_prep_body, tiles_per_seq=seq // tm, width=wd,
                          w_scale=float(n_idx_heads * IDX_DIM) ** -0.5),
        grid=(n // tm,),
        in_specs=[pl.BlockSpec((tm, d), row), pl.BlockSpec((1, d), fixed), pl.BlockSpec((d, ppad), fixed),
                  pl.BlockSpec(conv_w.shape, fixed)] + [pl.BlockSpec((tm, V7X_LANES), row)] * 3,
        out_specs=[pl.BlockSpec((tm, wd), row)] * 5 + [pl.BlockSpec((tm, V7X_LANES), row)],
        out_shape=outs,
        scratch_shapes=[pltpu.VMEM((tm + DN_KERNEL_ROWS, ppad), F32)],
        compiler_params=_cparams(("arbitrary",), 48),
        name="ev_prep",
    )(x, gain.reshape(1, d), w_in, conv_w, *tables)


COUNT_ROWS = 64


def _key_to_float(key):
    return pltpu.bitcast(key ^ ((key >> 31) & 0x7FFFFFFF), F32)


def _topk_mask_body(qi_ref, wt_ref, ki_ref, mask_ref, sc_ref, *, tq, tk, n_heads, top_k, seq):
    i = pl.program_id(1)
    n_kt = seq // tk
    q0 = i * tq
    n_act = (q0 + tq + tk - 1) // tk
    q_pos = q0 + lax.broadcasted_iota(I32, (1, tq), 1)
    key_iota = lax.broadcasted_iota(I32, (tk, tq), 0)
    wt = wt_ref[0]

    q_pairs = [jnp.concatenate([qi_ref[0, :, (2 * p) * IDX_DIM:(2 * p + 1) * IDX_DIM],
                                qi_ref[0, :, (2 * p + 1) * IDX_DIM:(2 * p + 2) * IDX_DIM]], axis=0)
               for p in range(n_heads // 2)]
    sub = V7X_LANES

    def score_tile(j, carry):
        for rb in range(tk // sub):
            kr = ki_ref[0, j, rb * sub:(rb + 1) * sub, :]
            pair_scores = [_dot_nt(kr, qp) for qp in q_pairs]
            acc = jnp.zeros((sub, tq), F32)
            for p, s2 in enumerate(pair_scores):
                acc = (acc + jnp.maximum(s2[:, :tq], 0.0) * wt[2 * p:2 * p + 1, :]
                       + jnp.maximum(s2[:, tq:], 0.0) * wt[2 * p + 1:2 * p + 2, :])
            key_pos = j * tk + rb * sub + key_iota[:sub]
            sc_ref[j, rb * sub:(rb + 1) * sub, :] = jnp.where(key_pos <= q_pos, acc, -jnp.inf)
        return carry

    lax.fori_loop(0, n_act, score_tile, 0)

    def count(pred):
        def body(j, acc):
            hit = jnp.where(pred(sc_ref[j], j), 1.0, 0.0)
            return acc + hit.reshape(tk // COUNT_ROWS, COUNT_ROWS, tq).sum(axis=0)
        acc = lax.fori_loop(0, n_act, body, jnp.zeros((COUNT_ROWS, tq), F32))
        return jnp.sum(acc, axis=0, keepdims=True)

    kf = float(top_k)
    n_adm = (q_pos + 1).astype(F32)
    n0 = count(lambda s, j: s >= 0.0)
    thr_key = jnp.where(n0 >= kf, 0, INT_MIN).astype(I32)
    n_ge = jnp.where(n0 >= kf, n0, n_adm)

    def try_bit(bit, key, n_keep):
        cand = key | jnp.left_shift(jnp.int32(1), bit)
        cf = _key_to_float(cand)
        n = count(lambda s, j: s >= cf)
        ok = n >= kf
        return jnp.where(ok, cand, key), jnp.where(ok, n, n_keep)

    group = 4
    top = 30
    while (top + 1) % group:
        thr_key, n_ge = try_bit(top, thr_key, n_ge)
        top -= 1

    def more(c):
        return jnp.logical_and(c[0] >= 0, jnp.max(c[2]) > kf)

    def bit_group(c):
        hi, key, n_keep = c
        for d in range(group):
            key, n_keep = try_bit(hi - d, key, n_keep)
        return hi - group, key, n_keep

    _, thr_key, n_ge = lax.while_loop(more, bit_group, (jnp.int32(top), thr_key, n_ge))
    few = n_adm <= kf
    thr = jnp.where(few, float(jnp.finfo(F32).min), _key_to_float(thr_key))
    n_ge = jnp.where(few, n_adm, n_ge)
    ambiguous = jnp.max(n_ge) > kf
    n_bits = max(1, (seq - 1).bit_length())

    def tie_cut():
        need = kf - count(lambda s, j: s > thr)

        def step(b, x):
            cand = x | jnp.left_shift(jnp.int32(1), n_bits - 1 - b)
            cnt = count(lambda s, j: jnp.where(s == thr, j * tk + key_iota, seq) < cand)
            return jnp.where(cnt < need, cand, x)
        return lax.fori_loop(0, n_bits, step, jnp.zeros((1, tq), I32))

    cut = lax.cond(ambiguous, tie_cut, lambda: jnp.full((1, tq), seq, I32))

    def write_mask(j, carry):
        s = sc_ref[j]
        tie_pos = jnp.where(s == thr, j * tk + key_iota, seq + 1)
        sel = jnp.where(s > thr, 1, jnp.where(tie_pos <= cut, 1, 0))
        mask_ref[0, j] = sel.astype(mask_ref.dtype)
        return carry

    def write_zero(j, carry):
        mask_ref[0, j] = jnp.zeros((tk, tq), mask_ref.dtype)
        return carry

    lax.fori_loop(0, n_act, write_mask, 0)
    lax.fori_loop(n_act, n_kt, write_zero, 0)


def _topk_mask(qi, wt, ki, top_k, tk):
    b, s, iw = qi.shape
    n_heads = iw // IDX_DIM
    tq = V7X_LANES
    n_kt = s // tk
    return pl.pallas_call(
        functools.partial(_topk_mask_body, tq=tq, tk=tk, n_heads=n_heads, top_k=top_k, seq=s),
        grid=(b, s // tq),
        in_specs=[pl.BlockSpec((1, tq, iw), lambda bi, i: (bi, i, 0)),
                  pl.BlockSpec((1, n_heads, tq), lambda bi, i: (bi, 0, i)),
                  pl.BlockSpec((1, n_kt, tk, IDX_DIM), lambda bi, i: (bi, 0, 0, 0))],
        out_specs=pl.BlockSpec((1, n_kt, tk, tq), lambda bi, i: (bi, 0, 0, i)),
        out_shape=jax.ShapeDtypeStruct((b, n_kt, tk, s), jnp.int8),
        scratch_shapes=[pltpu.VMEM((n_kt, tk, tq), F32)],
        compiler_params=_cparams(("parallel", "arbitrary"), 48),
        name="topk_mask",
    )(qi, wt, ki)


def _attn_body(i_tab, j_tab, q_ref, k_ref, vt_ref, mask_ref, o_ref, m_ref, l_ref, acc_ref, bias_ref, s_ref, *,
               tq, tk, n_heads):
    t = pl.program_id(1)
    i = i_tab[t]
    j = j_tab[t]
    hd = ATT_HEAD_DIM
    blk = V7X_LANES

    @pl.when(j == 0)
    def _():
        m_ref[...] = jnp.full(m_ref.shape, NEG_BIG, F32)
        l_ref[...] = jnp.zeros_like(l_ref)
        acc_ref[...] = jnp.zeros_like(acc_ref)

    bias_ref[...] = jnp.where(mask_ref[0, 0].astype(F32) != 0.0, 0.0, NEG_BIG)
    lane = lax.broadcasted_iota(I32, (1, 2 * hd), 1)
    half_sel = [jnp.where(lane < hd, 1.0, 0.0).astype(BF16), jnp.where(lane >= hd, 1.0, 0.0).astype(BF16)]

    key_blocks = [slice(kb * blk, (kb + 1) * blk) for kb in range(tk // blk)]
    ones_rows = jnp.ones((BF16_SUBLANES, blk), BF16)
    m_tile = {}
    base = jnp.minimum(j, 0)

    def scores(h):
        pair = slice((h // 2) * 2 * hd, (h // 2 + 1) * 2 * hd)
        qh = q_ref[0, :, pair] * half_sel[h % 2]
        top = None
        for ks in key_blocks:
            s = _dot_nt(k_ref[0, ks, pair], qh) + bias_ref[ks, :]
            s_ref[base + h, ks, :] = s
            bm = jnp.max(s, axis=0, keepdims=True)
            top = bm if top is None else jnp.maximum(top, bm)
        m_tile[h] = top

    def update(h):
        hs = slice(h * hd, (h + 1) * hd)
        m = m_ref[h:h + 1, :]
        m_new = jnp.maximum(m, m_tile[h])
        alpha = jnp.exp(m - m_new)
        pv = None
        for ks in key_blocks:
            p = jnp.exp(s_ref[base + h, ks, :] - m_new).astype(BF16)
            lhs = jnp.concatenate([vt_ref[0, hs, ks], ones_rows], axis=0)
            d = jnp.dot(lhs, p, preferred_element_type=F32)
            pv = d if pv is None else pv + d
        l_ref[h:h + 1, :] = alpha * l_ref[h:h + 1, :] + pv[hd:hd + 1, :]
        m_ref[h:h + 1, :] = m_new
        acc_ref[hs, :] = alpha * acc_ref[hs, :] + pv[:hd, :]

    for h in range(n_heads):
        scores(h)
    for h in range(n_heads):
        update(h)

    @pl.when(j == ((i + 1) * tq - 1) // tk)
    def _():
        for h in range(n_heads):
            hs = slice(h * hd, (h + 1) * hd)
            o_ref[0, hs, :] = (acc_ref[hs, :] / l_ref[h:h + 1, :]).astype(o_ref.dtype)


def _masked_attention(q, k, vt, mask, tk):
    b, s, aw = q.shape
    n_heads = aw // ATT_HEAD_DIM
    tq = _pick_tile(s, 256, V7X_LANES)
    pairs = [(i, j) for i in range(s // tq) for j in range(((i + 1) * tq - 1) // tk + 1)]
    i_tab = jnp.asarray([p[0] for p in pairs], I32)
    j_tab = jnp.asarray([p[1] for p in pairs], I32)
    grid_spec = pltpu.PrefetchScalarGridSpec(
        num_scalar_prefetch=2,
        grid=(b, len(pairs)),
        in_specs=[pl.BlockSpec((1, tq, aw), lambda bi, t, it, jt: (bi, it[t], 0)),
                  pl.BlockSpec((1, tk, aw), lambda bi, t, it, jt: (bi, jt[t], 0)),
                  pl.BlockSpec((1, aw, tk), lambda bi, t, it, jt: (bi, 0, jt[t])),
                  pl.BlockSpec((1, 1, tk, tq), lambda bi, t, it, jt: (bi, jt[t], 0, it[t]))],
        out_specs=pl.BlockSpec((1, aw, tq), lambda bi, t, it, jt: (bi, 0, it[t])),
        scratch_shapes=[pltpu.VMEM((n_heads, tq), F32), pltpu.VMEM((n_heads, tq), F32),
                        pltpu.VMEM((aw, tq), F32), pltpu.VMEM((tk, tq), F32),
                        pltpu.VMEM((n_heads, tk, tq), F32)],
    )
    return pl.pallas_call(
        functools.partial(_attn_body, tq=tq, tk=tk, n_heads=n_heads),
        grid_spec=grid_spec,
        out_shape=jax.ShapeDtypeStruct((b, aw, s), BF16),
        compiler_params=_cparams(("parallel", "arbitrary"), 32),
        name="masked_attn",
    )(i_tab, j_tab, q, k, vt, mask)


def _outproj_body(x_ref, *refs):
    n_in = (len(refs) - 1) // 2
    acc = x_ref[...]
    for y_ref, w_ref in zip(refs[:n_in], refs[n_in:2 * n_in]):
        acc = acc + jnp.dot(y_ref[...], w_ref[...], preferred_element_type=F32)
    refs[-1][...] = acc


def _outproj(x, ys, ws):
    n, d = x.shape
    tm = _pick_tile(n, 512, 8)
    row = lambda i: (i, 0)
    fixed = lambda i: (0, 0)
    return pl.pallas_call(
        _outproj_body,
        grid=(n // tm,),
        in_specs=[pl.BlockSpec((tm, d), row)] + [pl.BlockSpec((tm, y.shape[1]), row) for y in ys]
                 + [pl.BlockSpec(w.shape, fixed) for w in ws],
        out_specs=pl.BlockSpec((tm, d), row),
        out_shape=jax.ShapeDtypeStruct((n, d), F32),
        compiler_params=_cparams(("parallel",), 32),
        name="outproj",
    )(x, *ys, *ws)


def _od_prep_body(x_ref, g_ref, w_ref, cw_ref, alog_ref, dtb_ref,
                  q_ref, k_ref, v_ref, z_ref, bg_ref, p_ref, *, tiles_per_seq, n_heads):
    tm = x_ref.shape[0]
    hist = DN_KERNEL_ROWS
    d = n_heads * DN_HEAD_DIM
    _project_rows(x_ref, g_ref, w_ref, p_ref, tiles_per_seq)
    ktaps = cw_ref.shape[0]
    for part, out_ref in enumerate((q_ref, k_ref, v_ref)):
        for h in range(n_heads):
            col = part * d + h * DN_HEAD_DIM
            conv = None
            for j in range(ktaps):
                shift = ktaps - 1 - j
                term = (p_ref[hist - shift:hist - shift + tm, col:col + DN_HEAD_DIM]
                        * cw_ref[j:j + 1, col:col + DN_HEAD_DIM])
                conv = term if conv is None else conv + term
            u = _silu(conv)
            if part < 2:
                u = u * lax.rsqrt(jnp.sum(u * u, axis=-1, keepdims=True) + NORM_EPS)
            if part == 0:
                u = u * DN_HEAD_DIM ** -0.5
            out_ref[:, h * DN_HEAD_DIM:(h + 1) * DN_HEAD_DIM] = u
    z_ref[...] = p_ref[hist:hist + tm, 3 * d:4 * d]
    slab = p_ref[hist:hist + tm, 4 * d:4 * d + V7X_LANES]
    lane = lax.broadcasted_iota(I32, slab.shape, 1)
    xs = slab + dtb_ref[...]
    softplus = jnp.maximum(xs, 0.0) + jnp.log1p(jnp.exp(-jnp.abs(xs)))
    bg_ref[...] = jnp.where(lane < n_heads, jax.nn.sigmoid(slab), -jnp.exp(alog_ref[...]) * softplus)


def _od_prep(x, gain, w_in, conv_w, a_log, dt_bias, seq):
    n, d = x.shape
    n_heads = d // DN_HEAD_DIM
    ppad = w_in.shape[1]
    tm = _pick_tile(seq, 256, 8)
    row = lambda i: (i, 0)
    fixed = lambda i: (0, 0)
    pad = lambda t: jnp.pad(t.astype(F32), (n_heads, V7X_LANES - 2 * n_heads)).reshape(1, V7X_LANES)
    outs = [jax.ShapeDtypeStruct((n, d), F32)] * 4 + [jax.ShapeDtypeStruct((n, V7X_LANES), F32)]
    return pl.pallas_call(
        functools.partial(_od_prep_body, tiles_per_seq=seq // tm, n_heads=n_heads),
        grid=(n // tm,),
        in_specs=[pl.BlockSpec((tm, d), row), pl.BlockSpec((1, d), fixed), pl.BlockSpec((d, ppad), fixed),
                  pl.BlockSpec(conv_w.shape, fixed), pl.BlockSpec((1, V7X_LANES), fixed),
                  pl.BlockSpec((1, V7X_LANES), fixed)],
        out_specs=[pl.BlockSpec((tm, d), row)] * 4 + [pl.BlockSpec((tm, V7X_LANES), row)],
        out_shape=outs,
        scratch_shapes=[pltpu.VMEM((tm + DN_KERNEL_ROWS, ppad), F32)],
        compiler_params=_cparams(("arbitrary",), 56),
        name="od_prep",
    )(x, gain.reshape(1, d), w_in, conv_w, pad(a_log), pad(dt_bias))


def _delta_body(q_ref, k_ref, v_ref, z_ref, bg_ref, bgt_ref, gain_ref, y_ref, state_ref, *, n_heads, n_chunks):
    c_len = DN_CHUNK
    hd = DN_HEAD_DIM
    heads = range(n_heads)

    @pl.when(pl.program_id(1) == 0)
    def _():
        state_ref[...] = jnp.zeros_like(state_ref)

    ri = lax.broadcasted_iota(I32, (c_len, c_len), 0)
    ci = lax.broadcasted_iota(I32, (c_len, c_len), 1)
    lower = ci <= ri
    strict = ci < ri
    eye = jnp.where(ci == ri, 1.0, 0.0)

    group = max(g for g in (1, 2, 4) if n_chunks % g == 0)

    def group_step(cg, carry):
        chunk_rows = [pl.ds(pl.multiple_of((cg * group + g) * c_len, c_len), c_len) for g in range(group)]
        bg = [bg_ref[0, chunk_rows[g], :] for g in range(group)]
        bgt = [bgt_ref[0, cg * group + g] for g in range(group)]
        units = [(g, h) for g in range(group) for h in heads]
        cols = lambda h: slice(h * hd, (h + 1) * hd)
        q = [q_ref[0, chunk_rows[g], cols(h)] for g, h in units]
        k = [k_ref[0, chunk_rows[g], cols(h)] for g, h in units]
        v = [v_ref[0, chunk_rows[g], cols(h)] for g, h in units]
        n_units = range(len(units))
        beta = [bg[g][:, h:h + 1] for g, h in units]
        gc_col = [jnp.sum(jnp.where(lower, bgt[g][n_heads + h:n_heads + h + 1, :], 0.0), axis=1, keepdims=True)
                  for g, h in units]
        gc_row = [jnp.sum(jnp.where(ri <= ci, bg[g][:, n_heads + h:n_heads + h + 1], 0.0), axis=0, keepdims=True)
                  for g, h in units]
        decay = [jnp.where(lower, jnp.exp(jnp.where(lower, gc_col[u] - gc_row[u], 0.0)), 0.0) for u in n_units]
        kb = [k[u] * beta[u] for u in n_units]
        kbf = [k[u].astype(BF16) for u in n_units]
        kq = [_dot_nt(jnp.concatenate([kb[u], q[u]], axis=0).astype(BF16), kbf[u]) for u in n_units]
        intra = [(kq[u][c_len:] * decay[u]).astype(BF16) for u in n_units]
        pw = [jnp.where(strict, -(kq[u][:c_len] * decay[u]), 0.0) for u in n_units]
        t_inv = [eye + pw[u] for u in n_units]
        span = 1
        while 2 * span < c_len:
            pwb = [p.astype(BF16) for p in pw]
            pw = [jnp.dot(pwb[u], pwb[u], preferred_element_type=F32) for u in n_units]
            t_inv = [t_inv[u] + jnp.dot(t_inv[u].astype(BF16), pw[u].astype(BF16), preferred_element_type=F32)
                     for u in n_units]
            span *= 2
        e_gc = [jnp.exp(gc_col[u]) for u in n_units]
        uw = [jnp.dot(t_inv[u].astype(BF16),
                      jnp.concatenate([v[u] * beta[u], kb[u] * e_gc[u]], axis=1).astype(BF16),
                      preferred_element_type=F32) for u in n_units]
        g_last = [gc_col[u][c_len - 1:c_len, :] for u in n_units]
        k_dec_t = [(k[u] * jnp.exp(g_last[u] - gc_col[u])).T.astype(BF16) for u in n_units]
        wq = [jnp.concatenate([uw[u][:, hd:], q[u] * e_gc[u]], axis=0).astype(BF16) for u in n_units]
        state = [state_ref[h] for h in heads]
        for g in range(group):
            us = [g * n_heads + h for h in heads]
            ws = [jnp.dot(wq[u], state[h].astype(BF16), preferred_element_type=F32) for h, u in zip(heads, us)]
            v_new = [(uw[u][:, :hd] - ws[h][:c_len]).astype(BF16) for h, u in zip(heads, us)]
            out = [ws[h][c_len:] + jnp.dot(intra[u], v_new[h], preferred_element_type=F32)
                   for h, u in zip(heads, us)]
            state = [state[h] * jnp.exp(g_last[u]) + jnp.dot(k_dec_t[u], v_new[h], preferred_element_type=F32)
                     for h, u in zip(heads, us)]
            for h in heads:
                o = out[h] * lax.rsqrt(jnp.mean(out[h] * out[h], axis=-1, keepdims=True) + NORM_EPS) * gain_ref[...]
                zh = z_ref[0, chunk_rows[g], cols(h)]
                y_ref[0, chunk_rows[g], cols(h)] = (o * _silu(zh)).astype(y_ref.dtype)
        for h in heads:
            state_ref[h] = state[h]
        return carry

    lax.fori_loop(0, n_chunks // group, group_step, 0)


def _delta_rule(q, k, v, z, bg, bgt, o_gain):
    b, s, d = q.shape
    n_heads = d // DN_HEAD_DIM
    rows = _pick_tile(s, 512, DN_CHUNK)
    n_chunks = rows // DN_CHUNK
    blk = lambda bi, i: (bi, i, 0)
    return pl.pallas_call(
        functools.partial(_delta_body, n_heads=n_heads, n_chunks=n_chunks),
        grid=(b, s // rows),
        in_specs=[pl.BlockSpec((1, rows, d), blk)] * 4 + [
            pl.BlockSpec((1, rows, V7X_LANES), blk),
            pl.BlockSpec((1, n_chunks, 2 * n_heads, DN_CHUNK), lambda bi, i: (bi, i, 0, 0)),
            pl.BlockSpec((1, DN_HEAD_DIM), lambda bi, i: (0, 0))],
        out_specs=pl.BlockSpec((1, rows, d), blk),
        out_shape=jax.ShapeDtypeStruct((b, s, d), BF16),
        scratch_shapes=[pltpu.VMEM((n_heads, DN_HEAD_DIM, DN_HEAD_DIM), F32)],
        compiler_params=_cparams(("parallel", "arbitrary"), 40),
        name="delta_rule",
    )(q, k, v, z, bg, bgt, o_gain.reshape(1, DN_HEAD_DIM))


def _rope_tables(positions):
    half = ROPE_DIM // 2
    inv_freq = ROPE_THETA ** (-jnp.arange(0, ROPE_DIM, 2, dtype=F32) / ROPE_DIM)
    ang = positions.astype(F32).reshape(-1, 1) * inv_freq
    cos, sin = jnp.cos(ang), jnp.sin(ang)
    n = ang.shape[0]
    rest = ATT_HEAD_DIM - ROPE_DIM
    zeros = lambda m: jnp.zeros((n, m), F32)
    c = jnp.concatenate([cos, cos, jnp.ones((n, rest), F32)], axis=1)
    sa = jnp.concatenate([-sin, zeros(half + rest)], axis=1)
    sb = jnp.concatenate([zeros(half), sin, zeros(rest)], axis=1)
    reps = V7X_LANES // ATT_HEAD_DIM
    return tuple(jnp.tile(t, (1, reps)) for t in (c, sa, sb))


def _pad_cols(w, mult):
    pad = (-w.shape[-1]) % mult
    return jnp.pad(w, [(0, 0)] * (w.ndim - 1) + [(0, pad)])


def kernel(x, positions, norm_gain, ffn_w_gate, ffn_w_up, ffn_w_down, ev_w_in, ev_conv_w, ev_w_out,
           od_w_in, od_conv_w, od_a_log, od_dt_bias, od_o_gain, od_w_out, final_gain):
    bsz, seq, d = x.shape
    depth = norm_gain.shape[0]
    n = bsz * seq
    assert d % (2 * V7X_LANES) == 0 and seq % V7X_LANES == 0
    top_k = min(TOPK_MAX, seq // 4)
    tk = _pick_tile(seq, 512, V7X_LANES)
    dff = ffn_w_gate.shape[-1]

    wg = ffn_w_gate.reshape(depth * 2, d, dff)
    wu = ffn_w_up.reshape(depth * 2, d, dff)
    wd = ffn_w_down.reshape(depth * 2, dff, d)
    ev_in = _pad_cols(ev_w_in, V7X_LANES).astype(BF16)
    ev_out = ev_w_out.astype(BF16)
    od_in = _pad_cols(od_w_in, V7X_LANES).astype(BF16)
    od_out = od_w_out.astype(BF16)
    tables = _rope_tables(positions)
    half_w = d // 2

    x = x.reshape(n, d)
    for layer in range(depth):
        i = layer // 2
        x = _ffn(x, norm_gain[layer, 0], wg, wu, wd, 2 * layer)
        if layer % 2 == 0:
            yc, q, k, v, qi, kiw = _ev_prep(x, norm_gain[layer, 1], ev_in[i], ev_conv_w[i], tables, seq)
            kiw3 = kiw.reshape(bsz, seq, V7X_LANES)
            ki = kiw3[:, :, :IDX_DIM].astype(BF16).reshape(bsz, seq // tk, tk, IDX_DIM)
            n_idx = half_w // IDX_DIM
            wt = jnp.swapaxes(kiw3[:, :, IDX_DIM:IDX_DIM + n_idx], 1, 2)
            to3 = lambda t: t.reshape(bsz, seq, half_w)
            mask = _topk_mask(to3(qi), wt, ki, top_k, tk)
            ya_t = _masked_attention(to3(q), to3(k), jnp.swapaxes(to3(v), 1, 2), mask, tk)
            ya = jnp.swapaxes(ya_t, 1, 2).reshape(n, half_w)
            x = _outproj(x, [yc, ya], [ev_out[i, :half_w], ev_out[i, half_w:]])
        else:
            q, k, v, z, bg = _od_prep(x, norm_gain[layer, 1], od_in[i], od_conv_w[i], od_a_log[i],
                                      od_dt_bias[i], seq)
            n_heads = d // DN_HEAD_DIM
            bg3 = bg.reshape(bsz, seq, V7X_LANES)
            bgt = jnp.swapaxes(bg3[:, :, :2 * n_heads].reshape(bsz, seq // DN_CHUNK, DN_CHUNK, 2 * n_heads),
                               2, 3)
            to3 = lambda t: t.reshape(bsz, seq, d)
            y = _delta_rule(to3(q), to3(k), to3(v), to3(z), bg3, bgt, od_o_gain[i]).reshape(n, d)
            x = _outproj(x, [y], [od_out[i]])
        x = _ffn(x, norm_gain[layer, 2], wg, wu, wd, 2 * layer + 1,
                 final_gain=final_gain if layer == depth - 1 else None)
    return x.reshape(bsz, seq, d)
```

```python
import functools
import math

import jax
import jax.numpy as jnp
from jax import lax
from jax.experimental import pallas as pl
from jax.experimental.pallas import tpu as pltpu

F32 = jnp.float32
BF16 = jnp.bfloat16
I32 = jnp.int32

NORM_EPS = 1e-6
ATT_HEAD_DIM = 64
ATT_SCALE = ATT_HEAD_DIM ** -0.5
assert ATT_SCALE == 2.0 ** round(math.log2(ATT_SCALE)), "folding the scale into bf16 q needs a power of two"
IDX_DIM = 64
DN_HEAD_DIM = 128
DN_CHUNK = 64
DN_KERNEL_ROWS = 8
TOPK_MAX = 256
ROPE_DIM = 16
ROPE_THETA = 500000.0

V7X_LANES = 128
BF16_SUBLANES = 16
V7X_VMEM_BYTES = 64 * 2**20
INT_MIN = -(2**31)
NEG_BIG = -1e30


def _cparams(semantics, vmem_mb):
    assert vmem_mb * 2**20 < V7X_VMEM_BYTES
    return pltpu.CompilerParams(dimension_semantics=semantics, vmem_limit_bytes=vmem_mb * 2**20)


def _pick_tile(n, target, mult):
    best = None
    t = mult
    while t <= min(n, target):
        if n % t == 0:
            best = t
        t += mult
    assert best is not None, (n, target, mult)
    return best


def _rmsnorm(x, gain):
    ms = jnp.mean(x * x, axis=-1, keepdims=True)
    return x * lax.rsqrt(ms + NORM_EPS) * gain


def _silu(x):
    return x * jax.nn.sigmoid(x)


def _dot_nt(a, b):
    return lax.dot_general(a, b, (((1,), (1,)), ((), ())), preferred_element_type=F32)


def _ffn_body(x_ref, g_ref, wg_ref, wu_ref, wd_ref, *rest, final):
    if final:
        fg_ref, o_ref, h_ref, acc_ref = rest
    else:
        o_ref, h_ref, acc_ref = rest
    f = pl.program_id(1)

    @pl.when(f == 0)
    def _():
        h_ref[...] = _rmsnorm(x_ref[...], g_ref[...]).astype(BF16)
        acc_ref[...] = jnp.zeros_like(acc_ref)

    h = h_ref[...]
    a = jnp.dot(h, wg_ref[...].astype(BF16), preferred_element_type=F32)
    b = jnp.dot(h, wu_ref[...].astype(BF16), preferred_element_type=F32)
    t = (_silu(a) * b).astype(BF16)
    acc_ref[...] += jnp.dot(t, wd_ref[...].astype(BF16), preferred_element_type=F32)

    @pl.when(f == pl.num_programs(1) - 1)
    def _():
        y = x_ref[...] + 0.5 * acc_ref[...]
        if final:
            y = _rmsnorm(y, fg_ref[...])
        o_ref[...] = y


def _ffn(x, gain, wg, wu, wd, widx, final_gain=None):
    n, d = x.shape
    dff = wg.shape[-1]
    tm = _pick_tile(n, 1024, 8)
    tf = _pick_tile(dff, 512, V7X_LANES)
    final = final_gain is not None
    in_specs = [
        pl.BlockSpec((tm, d), lambda i, f: (i, 0)),
        pl.BlockSpec((1, d), lambda i, f: (0, 0)),
        pl.BlockSpec((None, d, tf), lambda i, f: (widx, 0, f)),
        pl.BlockSpec((None, d, tf), lambda i, f: (widx, 0, f)),
        pl.BlockSpec((None, tf, d), lambda i, f: (widx, f, 0)),
    ]
    args = [x, gain.reshape(1, d), wg, wu, wd]
    if final:
        in_specs.append(pl.BlockSpec((1, d), lambda i, f: (0, 0)))
        args.append(final_gain.reshape(1, d))
    return pl.pallas_call(
        functools.partial(_ffn_body, final=final),
        grid=(n // tm, dff // tf),
        in_specs=in_specs,
        out_specs=pl.BlockSpec((tm, d), lambda i, f: (i, 0)),
        out_shape=jax.ShapeDtypeStruct((n, d), F32),
        scratch_shapes=[pltpu.VMEM((tm, d), BF16), pltpu.VMEM((tm, d), F32)],
        compiler_params=_cparams(("parallel", "arbitrary"), 48),
        name="ffn_final" if final else "ffn",
    )(*args)


def _project_rows(x_ref, g_ref, w_ref, p_ref, tiles_per_seq):
    i = pl.program_id(0)
    tm = x_ref.shape[0]
    hist = DN_KERNEL_ROWS

    @pl.when(i % tiles_per_seq == 0)
    def _():
        p_ref[0:hist, :] = jnp.zeros((hist, p_ref.shape[1]), F32)

    @pl.when(i % tiles_per_seq != 0)
    def _():
        p_ref[0:hist, :] = p_ref[tm:tm + hist, :]

    h = _rmsnorm(x_ref[...], g_ref[...]).astype(BF16)
    p_ref[hist:hist + tm, :] = jnp.dot(h, w_ref[...], preferred_element_type=F32)


def _rope(x, c, sa, sb):
    n = x.shape[1]
    reps = n // V7X_LANES
    if reps > 1:
        c, sa, sb = (jnp.concatenate([t] * reps, axis=1) for t in (c, sa, sb))
    half = ROPE_DIM // 2
    x_up = pltpu.roll(x, n - half, 1)
    x_dn = pltpu.roll(x, half, 1)
    return x * c + x_up * sa + x_dn * sb


def _ev_prep_body(x_ref, g_ref, w_ref, cw_ref, c_ref, sa_ref, sb_ref,
                  yc_ref, q_ref, k_ref, v_ref, qi_ref, kiw_ref, p_ref, *, tiles_per_seq, width, w_scale):
    tm = x_ref.shape[0]
    hist = DN_KERNEL_ROWS
    _project_rows(x_ref, g_ref, w_ref, p_ref, tiles_per_seq)
    wd = width

    def rows(shift, col):
        return p_ref[hist - shift:hist - shift + tm, col * wd:(col + 1) * wd]

    ktaps = cw_ref.shape[0]
    conv = None
    for j in range(ktaps):
        shift = ktaps - 1 - j
        term = (rows(shift, 1) * rows(shift, 2)) * cw_ref[j:j + 1, :]
        conv = term if conv is None else conv + term
    yc_ref[...] = (rows(0, 0) * conv).astype(BF16)

    c, sa, sb = c_ref[...], sa_ref[...], sb_ref[...]
    q_ref[...] = (_rope(rows(0, 3), c, sa, sb) * ATT_SCALE).astype(BF16)
    k_ref[...] = _rope(rows(0, 4), c, sa, sb).astype(BF16)
    v_ref[...] = rows(0, 5).astype(BF16)
    qi_ref[...] = _rope(rows(0, 6), c, sa, sb).astype(BF16)
    slab = p_ref[hist:hist + tm, 7 * wd:7 * wd + V7X_LANES]
    lane = lax.broadcasted_iota(I32, slab.shape, 1)
    kiw_ref[...] = jnp.where(lane < IDX_DIM, _rope(slab, c, sa, sb), slab * w_scale)


def _ev_prep(x, gain, w_in, conv_w, tables, seq):
    n, d = x.shape
    wd = d // 2
    ppad = w_in.shape[1]
    tm = _pick_tile(seq, 256, 8)
    row = lambda i: (i, 0)
    fixed = lambda i: (0, 0)
    n_idx_heads = wd // IDX_DIM
    outs = [jax.ShapeDtypeStruct((n, wd), BF16)] * 5 + [jax.ShapeDtypeStruct((n, V7X_LANES), F32)]
    return pl.pallas_call(
        functools.partial(_ev_prep_body, tiles_per_seq=seq // tm, width=wd,
                          w_scale=float(n_idx_heads * IDX_DIM) ** -0.5),
        grid=(n // tm,),
        in_specs=[pl.BlockSpec((tm, d), row), pl.BlockSpec((1, d), fixed), pl.BlockSpec((d, ppad), fixed),
                  pl.BlockSpec(conv_w.shape, fixed)] + [pl.BlockSpec((tm, V7X_LANES), row)] * 3,
        out_specs=[pl.BlockSpec((tm, wd), row)] * 5 + [pl.BlockSpec((tm, V7X_LANES), row)],
        out_shape=outs,
        scratch_shapes=[pltpu.VMEM((tm + DN_KERNEL_ROWS, ppad), F32)],
        compiler_params=_cparams(("arbitrary",), 48),
        name="ev_prep",
    )(x, gain.reshape(1, d), w_in, conv_w, *tables)


COUNT_ROWS = 64
COUNT_ROWS16 = 128
HALF16 = 1 << 15


def _key_to_float(key):
    return pltpu.bitcast(key ^ ((key >> 31) & 0x7FFFFFFF), F32)


def _topk_mask_body(qi_ref, wt_ref, ki_ref, mask_ref, sc_ref, hi_ref, lo_ref, *, tq, tk, n_heads, top_k, seq):
    i = pl.program_id(1)
    n_kt = seq // tk
    q0 = i * tq
    n_act = (q0 + tq + tk - 1) // tk
    q_pos = q0 + lax.broadcasted_iota(I32, (1, tq), 1)
    key_iota = lax.broadcasted_iota(I32, (tk, tq), 0)
    wt = wt_ref[0]

    q_pairs = [jnp.concatenate([qi_ref[0, :, (2 * p) * IDX_DIM:(2 * p + 1) * IDX_DIM],
                                qi_ref[0, :, (2 * p + 1) * IDX_DIM:(2 * p + 2) * IDX_DIM]], axis=0)
               for p in range(n_heads // 2)]
    sub = V7X_LANES

    def score_tile(j, carry):
        for rb in range(tk // sub):
            kr = ki_ref[0, j, rb * sub:(rb + 1) * sub, :]
            pair_scores = [_dot_nt(kr, qp) for qp in q_pairs]
            acc = jnp.zeros((sub, tq), F32)
            for p, s2 in enumerate(pair_scores):
                acc = (acc + jnp.maximum(s2[:, :tq], 0.0) * wt[2 * p:2 * p + 1, :]
                       + jnp.maximum(s2[:, tq:], 0.0) * wt[2 * p + 1:2 * p + 2, :])
            admissible = j * tk + rb * sub + key_iota[:sub] <= q_pos
            acc = jnp.where(acc == 0.0, 0.0, acc)
            sc_ref[j, rb * sub:(rb + 1) * sub, :] = jnp.where(admissible, acc, -jnp.inf)
            bits = pltpu.bitcast(acc, I32)
            key = jnp.where(admissible, bits ^ ((bits >> 31) & 0x7FFFFFFF), INT_MIN)
            hi = key >> 16
            lo = (key & 0xFFFF) - HALF16
            half = sub // 2
            words = slice(rb * half, (rb + 1) * half)
            hi_ref[j, words, :] = (hi[:half] & 0xFFFF) | jnp.left_shift(hi[half:], 16)
            lo_ref[j, words, :] = (lo[:half] & 0xFFFF) | jnp.left_shift(lo[half:], 16)
        return carry

    lax.fori_loop(0, n_act, score_tile, 0)

    def packed16(c):
        word = (c & 0xFFFF) | jnp.left_shift(c, 16)
        return pltpu.bitcast(jnp.broadcast_to(word, (8, tq)), jnp.int16)[0:1, :]

    def count16(w_ref, c):
        c16 = packed16(c)
        one = jnp.ones((), jnp.int16)
        zero = jnp.zeros((), jnp.int16)

        def body(j, acc):
            hit = jnp.where(pltpu.bitcast(w_ref[j], jnp.int16) >= c16, one, zero)
            for r in range(tk // COUNT_ROWS16):
                acc = acc + hit[r * COUNT_ROWS16:(r + 1) * COUNT_ROWS16]
            return acc
        acc = lax.fori_loop(0, n_act, body, jnp.zeros((COUNT_ROWS16, tq), jnp.int16))
        return jnp.sum(acc.astype(I32).astype(F32), axis=0, keepdims=True)

    def count(pred):
        def body(j, acc):
            hit = jnp.where(pred(sc_ref[j], j), 1.0, 0.0)
            return acc + hit.reshape(tk // COUNT_ROWS, COUNT_ROWS, tq).sum(axis=0)
        acc = lax.fori_loop(0, n_act, body, jnp.zeros((COUNT_ROWS, tq), F32))
        return jnp.sum(acc, axis=0, keepdims=True)

    kf = float(top_k)
    n_adm = (q_pos + 1).astype(F32)
    def try_value(w_ref, cand, val, n_keep, n_above):
        n = n_above + count16(w_ref, cand)
        ok = n >= kf
        return jnp.where(ok, cand, val), jnp.where(ok, n, n_keep)

    low16 = jnp.full((1, tq), -HALF16, I32)
    t_hi, n_ge = try_value(hi_ref, jnp.zeros((1, tq), I32), low16, n_adm, 0.0)
    for bit in range(14, -1, -1):
        t_hi, n_ge = try_value(hi_ref, t_hi | (1 << bit), t_hi, n_ge, 0.0)
    n_above = jnp.where(t_hi < HALF16 - 1, count16(hi_ref, jnp.minimum(t_hi + 1, HALF16 - 1)), 0.0)
    t_hi16 = packed16(t_hi)

    def keep_equal_upper(j, carry):
        same = pltpu.bitcast(hi_ref[j], jnp.int16) == t_hi16
        lo = jnp.where(same, pltpu.bitcast(lo_ref[j], jnp.int16), jnp.full((), -HALF16, jnp.int16))
        lo_ref[j] = pltpu.bitcast(lo, I32)
        return carry

    lax.fori_loop(0, n_act, keep_equal_upper, 0)
    group = 4

    def more(c):
        return jnp.logical_and(c[0] >= 0, jnp.max(c[2]) > kf)

    def bit_group(c):
        top, val, n_keep = c
        for d in range(group):
            bit = jnp.maximum(top - d, 0)
            cand = jnp.where(top - d == 15, 0, val | jnp.left_shift(jnp.int32(1), bit))
            val, n_keep = try_value(lo_ref, cand, val, n_keep, n_above)
        return top - group, val, n_keep

    _, t_lo, n_ge = lax.while_loop(more, bit_group, (jnp.int32(15), low16, n_ge))
    thr_key = jnp.left_shift(t_hi, 16) | (t_lo + HALF16)
    few = n_adm <= kf
    thr = jnp.where(few, float(jnp.finfo(F32).min), _key_to_float(thr_key))
    n_ge = jnp.where(few, n_adm, n_ge)
    ambiguous = jnp.max(n_ge) > kf
    n_bits = max(1, (seq - 1).bit_length())

    def tie_cut():
        need = kf - count(lambda s, j: s > thr)

        def step(b, x):
            cand = x | jnp.left_shift(jnp.int32(1), n_bits - 1 - b)
            cnt = count(lambda s, j: jnp.where(s == thr, j * tk + key_iota, seq) < cand)
            return jnp.where(cnt < need, cand, x)
        return lax.fori_loop(0, n_bits, step, jnp.zeros((1, tq), I32))

    cut = lax.cond(ambiguous, tie_cut, lambda: jnp.full((1, tq), seq, I32))

    def write_mask(j, carry):
        s = sc_ref[j]
        tie_pos = jnp.where(s == thr, j * tk + key_iota, seq + 1)
        sel = jnp.where(s > thr, 1, jnp.where(tie_pos <= cut, 1, 0))
        mask_ref[0, j] = sel.astype(mask_ref.dtype)
        return carry

    def write_zero(j, carry):
        mask_ref[0, j] = jnp.zeros((tk, tq), mask_ref.dtype)
        return carry

    lax.fori_loop(0, n_act, write_mask, 0)
    lax.fori_loop(n_act, n_kt, write_zero, 0)


def _topk_mask(qi, wt, ki, top_k, tk):
    b, s, iw = qi.shape
    n_heads = iw // IDX_DIM
    tq = V7X_LANES
    n_kt = s // tk
    return pl.pallas_call(
        functools.partial(_topk_mask_body, tq=tq, tk=tk, n_heads=n_heads, top_k=top_k, seq=s),
        grid=(b, s // tq),
        in_specs=[pl.BlockSpec((1, tq, iw), lambda bi, i: (bi, i, 0)),
                  pl.BlockSpec((1, n_heads, tq), lambda bi, i: (bi, 0, i)),
                  pl.BlockSpec((1, n_kt, tk, IDX_DIM), lambda bi, i: (bi, 0, 0, 0))],
        out_specs=pl.BlockSpec((1, n_kt, tk, tq), lambda bi, i: (bi, 0, 0, i)),
        out_shape=jax.ShapeDtypeStruct((b, n_kt, tk, s), jnp.int8),
        scratch_shapes=[pltpu.VMEM((n_kt, tk, tq), F32), pltpu.VMEM((n_kt, tk // 2, tq), I32),
                        pltpu.VMEM((n_kt, tk // 2, tq), I32)],
        compiler_params=_cparams(("parallel", "arbitrary"), 48),
        name="topk_mask",
    )(qi, wt, ki)


def _attn_body(i_tab, j_tab, q_ref, k_ref, vt_ref, mask_ref, o_ref, m_ref, l_ref, acc_ref, bias_ref, s_ref, *,
               tq, tk, n_heads):
    t = pl.program_id(1)
    i = i_tab[t]
    j = j_tab[t]
    hd = ATT_HEAD_DIM
    blk = V7X_LANES

    @pl.when(j == 0)
    def _():
        m_ref[...] = jnp.full(m_ref.shape, NEG_BIG, F32)
        l_ref[...] = jnp.zeros_like(l_ref)
        acc_ref[...] = jnp.zeros_like(acc_ref)

    bias_ref[...] = jnp.where(mask_ref[0, 0].astype(F32) != 0.0, 0.0, NEG_BIG)
    lane = lax.broadcasted_iota(I32, (1, 2 * hd), 1)
    half_sel = [jnp.where(lane < hd, 1.0, 0.0).astype(BF16), jnp.where(lane >= hd, 1.0, 0.0).astype(BF16)]

    key_blocks = [slice(kb * blk, (kb + 1) * blk) for kb in range(tk // blk)]
    ones_rows = jnp.ones((BF16_SUBLANES, blk), BF16)
    m_tile = {}
    base = jnp.minimum(j, 0)

    def scores(h):
        pair = slice((h // 2) * 2 * hd, (h // 2 + 1) * 2 * hd)
        qh = q_ref[0, :, pair] * half_sel[h % 2]
        top = None
        for ks in key_blocks:
            s = _dot_nt(k_ref[0, ks, pair], qh) + bias_ref[ks, :]
            s_ref[base + h, ks, :] = s
            bm = jnp.max(s, axis=0, keepdims=True)
            top = bm if top is None else jnp.maximum(top, bm)
        m_tile[h] = top

    def update(h):
        hs = slice(h * hd, (h + 1) * hd)
        m = m_ref[h:h + 1, :]
        m_new = jnp.maximum(m, m_tile[h])
        alpha = jnp.exp(m - m_new)
        pv = None
        for ks in key_blocks:
            p = jnp.exp(s_ref[base + h, ks, :] - m_new).astype(BF16)
            lhs = jnp.concatenate([vt_ref[0, hs, ks], ones_rows], axis=0)
            d = jnp.dot(lhs, p, preferred_element_type=F32)
            pv = d if pv is None else pv + d
        l_ref[h:h + 1, :] = alpha * l_ref[h:h + 1, :] + pv[hd:hd + 1, :]
        m_ref[h:h + 1, :] = m_new
        acc_ref[hs, :] = alpha * acc_ref[hs, :] + pv[:hd, :]

    for h in range(n_heads):
        scores(h)
    for h in range(n_heads):
        update(h)

    @pl.when(j == ((i + 1) * tq - 1) // tk)
    def _():
        for h in range(n_heads):
            hs = slice(h * hd, (h + 1) * hd)
            o_ref[0, hs, :] = (acc_ref[hs, :] / l_ref[h:h + 1, :]).astype(o_ref.dtype)


def _masked_attention(q, k, vt, mask, tk):
    b, s, aw = q.shape
    n_heads = aw // ATT_HEAD_DIM
    tq = _pick_tile(s, 256, V7X_LANES)
    pairs = [(i, j) for i in range(s // tq) for j in range(((i + 1) * tq - 1) // tk + 1)]
    i_tab = jnp.asarray([p[0] for p in pairs], I32)
    j_tab = jnp.asarray([p[1] for p in pairs], I32)
    grid_spec = pltpu.PrefetchScalarGridSpec(
        num_scalar_prefetch=2,
        grid=(b, len(pairs)),
        in_specs=[pl.BlockSpec((1, tq, aw), lambda bi, t, it, jt: (bi, it[t], 0)),
                  pl.BlockSpec((1, tk, aw), lambda bi, t, it, jt: (bi, jt[t], 0)),
                  pl.BlockSpec((1, aw, tk), lambda bi, t, it, jt: (bi, 0, jt[t])),
                  pl.BlockSpec((1, 1, tk, tq), lambda bi, t, it, jt: (bi, jt[t], 0, it[t]))],
        out_specs=pl.BlockSpec((1, aw, tq), lambda bi, t, it, jt: (bi, 0, it[t])),
        scratch_shapes=[pltpu.VMEM((n_heads, tq), F32), pltpu.VMEM((n_heads, tq), F32),
                        pltpu.VMEM((aw, tq), F32), pltpu.VMEM((tk, tq), F32),
                        pltpu.VMEM((n_heads, tk, tq), F32)],
    )
    return pl.pallas_call(
        functools.partial(_attn_body, tq=tq, tk=tk, n_heads=n_heads),
        grid_spec=grid_spec,
        out_shape=jax.ShapeDtypeStruct((b, aw, s), BF16),
        compiler_params=_cparams(("parallel", "arbitrary"), 32),
        name="masked_attn",
    )(i_tab, j_tab, q, k, vt, mask)


def _outproj_body(x_ref, *refs):
    n_in = (len(refs) - 1) // 2
    acc = x_ref[...]
    for y_ref, w_ref in zip(refs[:n_in], refs[n_in:2 * n_in]):
        acc = acc + jnp.dot(y_ref[...], w_ref[...], preferred_element_type=F32)
    refs[-1][...] = acc


def _outproj(x, ys, ws):
    n, d = x.shape
    tm = _pick_tile(n, 512, 8)
    row = lambda i: (i, 0)
    fixed = lambda i: (0, 0)
    return pl.pallas_call(
        _outproj_body,
        grid=(n // tm,),
        in_specs=[pl.BlockSpec((tm, d), row)] + [pl.BlockSpec((tm, y.shape[1]), row) for y in ys]
                 + [pl.BlockSpec(w.shape, fixed) for w in ws],
        out_specs=pl.BlockSpec((tm, d), row),
        out_shape=jax.ShapeDtypeStruct((n, d), F32),
        compiler_params=_cparams(("parallel",), 32),
        name="outproj",
    )(x, *ys, *ws)


def _od_prep_body(x_ref, g_ref, w_ref, cw_ref, alog_ref, dtb_ref,
                  q_ref, k_ref, v_ref, z_ref, bg_ref, p_ref, *, tiles_per_seq, n_heads):
    tm = x_ref.shape[0]
    hist = DN_KERNEL_ROWS
    d = n_heads * DN_HEAD_DIM
    _project_rows(x_ref, g_ref, w_ref, p_ref, tiles_per_seq)
    ktaps = cw_ref.shape[0]
    for part, out_ref in enumerate((q_ref, k_ref, v_ref)):
        for h in range(n_heads):
            col = part * d + h * DN_HEAD_DIM
            conv = None
            for j in range(ktaps):
                shift = ktaps - 1 - j
                term = (p_ref[hist - shift:hist - shift + tm, col:col + DN_HEAD_DIM]
                        * cw_ref[j:j + 1, col:col + DN_HEAD_DIM])
                conv = term if conv is None else conv + term
            u = _silu(conv)
            if part < 2:
                u = u * lax.rsqrt(jnp.sum(u * u, axis=-1, keepdims=True) + NORM_EPS)
            if part == 0:
                u = u * DN_HEAD_DIM ** -0.5
            out_ref[:, h * DN_HEAD_DIM:(h + 1) * DN_HEAD_DIM] = u
    z_ref[...] = p_ref[hist:hist + tm, 3 * d:4 * d]
    slab = p_ref[hist:hist + tm, 4 * d:4 * d + V7X_LANES]
    lane = lax.broadcasted_iota(I32, slab.shape, 1)
    xs = slab + dtb_ref[...]
    softplus = jnp.maximum(xs, 0.0) + jnp.log1p(jnp.exp(-jnp.abs(xs)))
    bg_ref[...] = jnp.where(lane < n_heads, jax.nn.sigmoid(slab), -jnp.exp(alog_ref[...]) * softplus)


def _od_prep(x, gain, w_in, conv_w, a_log, dt_bias, seq):
    n, d = x.shape
    n_heads = d // DN_HEAD_DIM
    ppad = w_in.shape[1]
    tm = _pick_tile(seq, 256, 8)
    row = lambda i: (i, 0)
    fixed = lambda i: (0, 0)
    pad = lambda t: jnp.pad(t.astype(F32), (n_heads, V7X_LANES - 2 * n_heads)).reshape(1, V7X_LANES)
    outs = [jax.ShapeDtypeStruct((n, d), F32)] * 4 + [jax.ShapeDtypeStruct((n, V7X_LANES), F32)]
    return pl.pallas_call(
        functools.partial(_od_prep_body, tiles_per_seq=seq // tm, n_heads=n_heads),
        grid=(n // tm,),
        in_specs=[pl.BlockSpec((tm, d), row), pl.BlockSpec((1, d), fixed), pl.BlockSpec((d, ppad), fixed),
                  pl.BlockSpec(conv_w.shape, fixed), pl.BlockSpec((1, V7X_LANES), fixed),
                  pl.BlockSpec((1, V7X_LANES), fixed)],
        out_specs=[pl.BlockSpec((tm, d), row)] * 4 + [pl.BlockSpec((tm, V7X_LANES), row)],
        out_shape=outs,
        scratch_shapes=[pltpu.VMEM((tm + DN_KERNEL_ROWS, ppad), F32)],
        compiler_params=_cparams(("arbitrary",), 56),
        name="od_prep",
    )(x, gain.reshape(1, d), w_in, conv_w, pad(a_log), pad(dt_bias))


def _delta_body(q_ref, k_ref, v_ref, z_ref, bg_ref, bgt_ref, gain_ref, y_ref, state_ref, *, n_heads, n_chunks):
    c_len = DN_CHUNK
    hd = DN_HEAD_DIM
    heads = range(n_heads)

    @pl.when(pl.program_id(1) == 0)
    def _():
        state_ref[...] = jnp.zeros_like(state_ref)

    ri = lax.broadcasted_iota(I32, (c_len, c_len), 0)
    ci = lax.broadcasted_iota(I32, (c_len, c_len), 1)
    lower = ci <= ri
    strict = ci < ri
    eye = jnp.where(ci == ri, 1.0, 0.0)

    group = max(g for g in (1, 2, 4) if n_chunks % g == 0)

    def group_step(cg, carry):
        chunk_rows = [pl.ds(pl.multiple_of((cg * group + g) * c_len, c_len), c_len) for g in range(group)]
        bg = [bg_ref[0, chunk_rows[g], :] for g in range(group)]
        bgt = [bgt_ref[0, cg * group + g] for g in range(group)]
        units = [(g, h) for g in range(group) for h in heads]
        cols = lambda h: slice(h * hd, (h + 1) * hd)
        q = [q_ref[0, chunk_rows[g], cols(h)] for g, h in units]
        k = [k_ref[0, chunk_rows[g], cols(h)] for g, h in units]
        v = [v_ref[0, chunk_rows[g], cols(h)] for g, h in units]
        n_units = range(len(units))
        beta = [bg[g][:, h:h + 1] for g, h in units]
        gc_col = [jnp.sum(jnp.where(lower, bgt[g][n_heads + h:n_heads + h + 1, :], 0.0), axis=1, keepdims=True)
                  for g, h in units]
        gc_row = [jnp.sum(jnp.where(ri <= ci, bg[g][:, n_heads + h:n_heads + h + 1], 0.0), axis=0, keepdims=True)
                  for g, h in units]
        decay = [jnp.where(lower, jnp.exp(jnp.where(lower, gc_col[u] - gc_row[u], 0.0)), 0.0) for u in n_units]
        kb = [k[u] * beta[u] for u in n_units]
        kbf = [k[u].astype(BF16) for u in n_units]
        kq = [_dot_nt(jnp.concatenate([kb[u], q[u]], axis=0).astype(BF16), kbf[u]) for u in n_units]
        intra = [(kq[u][c_len:] * decay[u]).astype(BF16) for u in n_units]
        pw = [jnp.where(strict, -(kq[u][:c_len] * decay[u]), 0.0) for u in n_units]
        t_inv = [eye + pw[u] for u in n_units]
        span = 1
        while 2 * span < c_len:
            pwb = [p.astype(BF16) for p in pw]
            pw = [jnp.dot(pwb[u], pwb[u], preferred_element_type=F32) for u in n_units]
            t_inv = [t_inv[u] + jnp.dot(t_inv[u].astype(BF16), pw[u].astype(BF16), preferred_element_type=F32)
                     for u in n_units]
            span *= 2
        e_gc = [jnp.exp(gc_col[u]) for u in n_units]
        uw = [jnp.dot(t_inv[u].astype(BF16),
                      jnp.concatenate([v[u] * beta[u], kb[u] * e_gc[u]], axis=1).astype(BF16),
                      preferred_element_type=F32) for u in n_units]
        g_last = [gc_col[u][c_len - 1:c_len, :] for u in n_units]
        k_dec_t = [(k[u] * jnp.exp(g_last[u] - gc_col[u])).T.astype(BF16) for u in n_units]
        wq = [jnp.concatenate([uw[u][:, hd:], q[u] * e_gc[u]], axis=0).astype(BF16) for u in n_units]
        state = [state_ref[h] for h in heads]
        for g in range(group):
            us = [g * n_heads + h for h in heads]
            ws = [jnp.dot(wq[u], state[h].astype(BF16), preferred_element_type=F32) for h, u in zip(heads, us)]
            v_new = [(uw[u][:, :hd] - ws[h][:c_len]).astype(BF16) for h, u in zip(heads, us)]
            out = [ws[h][c_len:] + jnp.dot(intra[u], v_new[h], preferred_element_type=F32)
                   for h, u in zip(heads, us)]
            state = [state[h] * jnp.exp(g_last[u]) + jnp.dot(k_dec_t[u], v_new[h], preferred_element_type=F32)
                     for h, u in zip(heads, us)]
            for h in heads:
                o = out[h] * lax.rsqrt(jnp.mean(out[h] * out[h], axis=-1, keepdims=True) + NORM_EPS) * gain_ref[...]
                zh = z_ref[0, chunk_rows[g], cols(h)]
                y_ref[0, chunk_rows[g], cols(h)] = (o * _silu(zh)).astype(y_ref.dtype)
        for h in heads:
            state_ref[h] = state[h]
        return carry

    lax.fori_loop(0, n_chunks // group, group_step, 0)


def _delta_rule(q, k, v, z, bg, bgt, o_gain):
    b, s, d = q.shape
    n_heads = d // DN_HEAD_DIM
    rows = _pick_tile(s, 512, DN_CHUNK)
    n_chunks = rows // DN_CHUNK
    blk = lambda bi, i: (bi, i, 0)
    return pl.pallas_call(
        functools.partial(_delta_body, n_heads=n_heads, n_chunks=n_chunks),
        grid=(b, s // rows),
        in_specs=[pl.BlockSpec((1, rows, d), blk)] * 4 + [
            pl.BlockSpec((1, rows, V7X_LANES), blk),
            pl.BlockSpec((1, n_chunks, 2 * n_heads, DN_CHUNK), lambda bi, i: (bi, i, 0, 0)),
            pl.BlockSpec((1, DN_HEAD_DIM), lambda bi, i: (0, 0))],
        out_specs=pl.BlockSpec((1, rows, d), blk),
        out_shape=jax.ShapeDtypeStruct((b, s, d), BF16),
        scratch_shapes=[pltpu.VMEM((n_heads, DN_HEAD_DIM, DN_HEAD_DIM), F32)],
        compiler_params=_cparams(("parallel", "arbitrary"), 40),
        name="delta_rule",
    )(q, k, v, z, bg, bgt, o_gain.reshape(1, DN_HEAD_DIM))


def _rope_tables(positions):
    half = ROPE_DIM // 2
    inv_freq = ROPE_THETA ** (-jnp.arange(0, ROPE_DIM, 2, dtype=F32) / ROPE_DIM)
    ang = positions.astype(F32).reshape(-1, 1) * inv_freq
    cos, sin = jnp.cos(ang), jnp.sin(ang)
    n = ang.shape[0]
    rest = ATT_HEAD_DIM - ROPE_DIM
    zeros = lambda m: jnp.zeros((n, m), F32)
    c = jnp.concatenate([cos, cos, jnp.ones((n, rest), F32)], axis=1)
    sa = jnp.concatenate([-sin, zeros(half + rest)], axis=1)
    sb = jnp.concatenate([zeros(half), sin, zeros(rest)], axis=1)
    reps = V7X_LANES // ATT_HEAD_DIM
    return tuple(jnp.tile(t, (1, reps)) for t in (c, sa, sb))


def _pad_cols(w, mult):
    pad = (-w.shape[-1]) % mult
    return jnp.pad(w, [(0, 0)] * (w.ndim - 1) + [(0, pad)])


def kernel(x, positions, norm_gain, ffn_w_gate, ffn_w_up, ffn_w_down, ev_w_in, ev_conv_w, ev_w_out,
           od_w_in, od_conv_w, od_a_log, od_dt_bias, od_o_gain, od_w_out, final_gain):
    bsz, seq, d = x.shape
    depth = norm_gain.shape[0]
    n = bsz * seq
    assert d % (2 * V7X_LANES) == 0 and seq % V7X_LANES == 0
    top_k = min(TOPK_MAX, seq // 4)
    tk = _pick_tile(seq, 512, V7X_LANES)
    dff = ffn_w_gate.shape[-1]

    wg = ffn_w_gate.reshape(depth * 2, d, dff)
    wu = ffn_w_up.reshape(depth * 2, d, dff)
    wd = ffn_w_down.reshape(depth * 2, dff, d)
    ev_in = _pad_cols(ev_w_in, V7X_LANES).astype(BF16)
    ev_out = ev_w_out.astype(BF16)
    od_in = _pad_cols(od_w_in, V7X_LANES).astype(BF16)
    od_out = od_w_out.astype(BF16)
    tables = _rope_tables(positions)
    half_w = d // 2

    x = x.reshape(n, d)
    for layer in range(depth):
        i = layer // 2
        x = _ffn(x, norm_gain[layer, 0], wg, wu, wd, 2 * layer)
        if layer % 2 == 0:
            yc, q, k, v, qi, kiw = _ev_prep(x, norm_gain[layer, 1], ev_in[i], ev_conv_w[i], tables, seq)
            kiw3 = kiw.reshape(bsz, seq, V7X_LANES)
            ki = kiw3[:, :, :IDX_DIM].astype(BF16).reshape(bsz, seq // tk, tk, IDX_DIM)
            n_idx = half_w // IDX_DIM
            wt = jnp.swapaxes(kiw3[:, :, IDX_DIM:IDX_DIM + n_idx], 1, 2)
            to3 = lambda t: t.reshape(bsz, seq, half_w)
            mask = _topk_mask(to3(qi), wt, ki, top_k, tk)
            ya_t = _masked_attention(to3(q), to3(k), jnp.swapaxes(to3(v), 1, 2), mask, tk)
            ya = jnp.swapaxes(ya_t, 1, 2).reshape(n, half_w)
            x = _outproj(x, [yc, ya], [ev_out[i, :half_w], ev_out[i, half_w:]])
        else:
            q, k, v, z, bg = _od_prep(x, norm_gain[layer, 1], od_in[i], od_conv_w[i], od_a_log[i],
                                      od_dt_bias[i], seq)
            n_heads = d // DN_HEAD_DIM
            bg3 = bg.reshape(bsz, seq, V7X_LANES)
            bgt = jnp.swapaxes(bg3[:, :, :2 * n_heads].reshape(bsz, seq // DN_CHUNK, DN_CHUNK, 2 * n_heads),
                               2, 3)
            to3 = lambda t: t.reshape(bsz, seq, d)
            y = _delta_rule(to3(q), to3(k), to3(v), to3(z), bg3, bgt, od_o_gain[i]).reshape(n, d)
            x = _outproj(x, [y], [od_out[i]])
        x = _ffn(x, norm_gain[layer, 2], wg, wu, wd, 2 * layer + 1,
                 final_gain=final_gain if layer == depth - 1 else None)
    return x.reshape(bsz, seq, d)
```

```python
import functools
import math

import jax
import jax.numpy as jnp
from jax import lax
from jax.experimental import pallas as pl
from jax.experimental.pallas import tpu as pltpu

F32 = jnp.float32
BF16 = jnp.bfloat16
I32 = jnp.int32

NORM_EPS = 1e-6
ATT_HEAD_DIM = 64
ATT_SCALE = ATT_HEAD_DIM ** -0.5
assert ATT_SCALE == 2.0 ** round(math.log2(ATT_SCALE)), "folding the scale into bf16 q needs a power of two"
IDX_DIM = 64
DN_HEAD_DIM = 128
DN_CHUNK = 64
DN_KERNEL_ROWS = 8
TOPK_MAX = 256
ROPE_DIM = 16
ROPE_THETA = 500000.0

V7X_LANES = 128
BF16_SUBLANES = 16
V7X_VMEM_BYTES = 64 * 2**20
INT_MIN = -(2**31)
NEG_BIG = -1e30


def _cparams(semantics, vmem_mb):
    assert vmem_mb * 2**20 < V7X_VMEM_BYTES
    return pltpu.CompilerParams(dimension_semantics=semantics, vmem_limit_bytes=vmem_mb * 2**20)


def _pick_tile(n, target, mult):
    best = None
    t = mult
    while t <= min(n, target):
        if n % t == 0:
            best = t
        t += mult
    assert best is not None, (n, target, mult)
    return best


def _rmsnorm(x, gain):
    ms = jnp.mean(x * x, axis=-1, keepdims=True)
    return x * lax.rsqrt(ms + NORM_EPS) * gain


def _silu(x):
    return x * jax.nn.sigmoid(x)


def _dot_nt(a, b):
    return lax.dot_general(a, b, (((1,), (1,)), ((), ())), preferred_element_type=F32)


def _ffn_body(x_ref, g_ref, wg_ref, wu_ref, wd_ref, *rest, final):
    if final:
        fg_ref, o_ref, h_ref, acc_ref = rest
    else:
        o_ref, h_ref, acc_ref = rest
    f = pl.program_id(1)

    @pl.when(f == 0)
    def _():
        h_ref[...] = _rmsnorm(x_ref[...], g_ref[...]).astype(BF16)
        acc_ref[...] = jnp.zeros_like(acc_ref)

    h = h_ref[...]
    a = jnp.dot(h, wg_ref[...].astype(BF16), preferred_element_type=F32)
    b = jnp.dot(h, wu_ref[...].astype(BF16), preferred_element_type=F32)
    t = (_silu(a) * b).astype(BF16)
    acc_ref[...] += jnp.dot(t, wd_ref[...].astype(BF16), preferred_element_type=F32)

    @pl.when(f == pl.num_programs(1) - 1)
    def _():
        y = x_ref[...] + 0.5 * acc_ref[...]
        if final:
            y = _rmsnorm(y, fg_ref[...])
        o_ref[...] = y


def _ffn(x, gain, wg, wu, wd, widx, final_gain=None):
    n, d = x.shape
    dff = wg.shape[-1]
    tm = _pick_tile(n, 1024, 8)
    tf = _pick_tile(dff, 512, V7X_LANES)
    final = final_gain is not None
    in_specs = [
        pl.BlockSpec((tm, d), lambda i, f: (i, 0)),
        pl.BlockSpec((1, d), lambda i, f: (0, 0)),
        pl.BlockSpec((None, d, tf), lambda i, f: (widx, 0, f)),
        pl.BlockSpec((None, d, tf), lambda i, f: (widx, 0, f)),
        pl.BlockSpec((None, tf, d), lambda i, f: (widx, f, 0)),
    ]
    args = [x, gain.reshape(1, d), wg, wu, wd]
    if final:
        in_specs.append(pl.BlockSpec((1, d), lambda i, f: (0, 0)))
        args.append(final_gain.reshape(1, d))
    return pl.pallas_call(
        functools.partial(_ffn_body, final=final),
        grid=(n // tm, dff // tf),
        in_specs=in_specs,
        out_specs=pl.BlockSpec((tm, d), lambda i, f: (i, 0)),
        out_shape=jax.ShapeDtypeStruct((n, d), F32),
        scratch_shapes=[pltpu.VMEM((tm, d), BF16), pltpu.VMEM((tm, d), F32)],
        compiler_params=_cparams(("parallel", "arbitrary"), 48),
        name="ffn_final" if final else "ffn",
    )(*args)


def _project_rows(x_ref, g_ref, w_ref, p_ref, tiles_per_seq):
    i = pl.program_id(0)
    tm = x_ref.shape[0]
    hist = DN_KERNEL_ROWS

    @pl.when(i % tiles_per_seq == 0)
    def _():
        p_ref[0:hist, :] = jnp.zeros((hist, p_ref.shape[1]), F32)

    @pl.when(i % tiles_per_seq != 0)
    def _():
        p_ref[0:hist, :] = p_ref[tm:tm + hist, :]

    h = _rmsnorm(x_ref[...], g_ref[...]).astype(BF16)
    p_ref[hist:hist + tm, :] = jnp.dot(h, w_ref[...], preferred_element_type=F32)


def _rope(x, c, sa, sb):
    n = x.shape[1]
    reps = n // V7X_LANES
    if reps > 1:
        c, sa, sb = (jnp.concatenate([t] * reps, axis=1) for t in (c, sa, sb))
    half = ROPE_DIM // 2
    x_up = pltpu.roll(x, n - half, 1)
    x_dn = pltpu.roll(x, half, 1)
    return x * c + x_up * sa + x_dn * sb


def _ev_prep_body(x_ref, g_ref, w_ref, cw_ref, c_ref, sa_ref, sb_ref,
                  yc_ref, q_ref, k_ref, v_ref, qi_ref, kiw_ref, p_ref, *, tiles_per_seq, width, w_scale):
    tm = x_ref.shape[0]
    hist = DN_KERNEL_ROWS
    _project_rows(x_ref, g_ref, w_ref, p_ref, tiles_per_seq)
    wd = width

    def rows(shift, col):
        return p_ref[hist - shift:hist - shift + tm, col * wd:(col + 1) * wd]

    ktaps = cw_ref.shape[0]
    conv = None
    for j in range(ktaps):
        shift = ktaps - 1 - j
        term = (rows(shift, 1) * rows(shift, 2)) * cw_ref[j:j + 1, :]
        conv = term if conv is None else conv + term
    yc_ref[...] = (rows(0, 0) * conv).astype(BF16)

    c, sa, sb = c_ref[...], sa_ref[...], sb_ref[...]
    q_ref[...] = (_rope(rows(0, 3), c, sa, sb) * ATT_SCALE).astype(BF16)
    k_ref[...] = _rope(rows(0, 4), c, sa, sb).astype(BF16)
    v_ref[...] = rows(0, 5).astype(BF16)
    qi_ref[...] = _rope(rows(0, 6), c, sa, sb).astype(BF16)
    slab = p_ref[hist:hist + tm, 7 * wd:7 * wd + V7X_LANES]
    lane = lax.broadcasted_iota(I32, slab.shape, 1)
    kiw_ref[...] = jnp.where(lane < IDX_DIM, _rope(slab, c, sa, sb), slab * w_scale)


def _ev_prep(x, gain, w_in, conv_w, tables, seq):
    n, d = x.shape
    wd = d // 2
    ppad = w_in.shape[1]
    tm = _pick_tile(seq, 256, 8)
    row = lambda i: (i, 0)
    fixed = lambda i: (0, 0)
    n_idx_heads = wd // IDX_DIM
    outs = [jax.ShapeDtypeStruct((n, wd), BF16)] * 5 + [jax.ShapeDtypeStruct((n, V7X_LANES), F32)]
    return pl.pallas_call(
        functools.partial(_ev_prep_body, tiles_per_seq=seq // tm, width=wd,
                          w_scale=float(n_idx_heads * IDX_DIM) ** -0.5),
        grid=(n // tm,),
        in_specs=[pl.BlockSpec((tm, d), row), pl.BlockSpec((1, d), fixed), pl.BlockSpec((d, ppad), fixed),
                  pl.BlockSpec(conv_w.shape, fixed)] + [pl.BlockSpec((tm, V7X_LANES), row)] * 3,
        out_specs=[pl.BlockSpec((tm, wd), row)] * 5 + [pl.BlockSpec((tm, V7X_LANES), row)],
        out_shape=outs,
        scratch_shapes=[pltpu.VMEM((tm + DN_KERNEL_ROWS, ppad), F32)],
        compiler_params=_cparams(("arbitrary",), 48),
        name="ev_prep",
    )(x, gain.reshape(1, d), w_in, conv_w, *tables)


COUNT_ROWS = 64
COUNT_ROWS16 = 64
HALF16 = 1 << 15
FLIP_BIT15_BOTH = (HALF16 | HALF16 << 16) - (1 << 32)


def _key_to_float(key):
    return pltpu.bitcast(key ^ ((key >> 31) & 0x7FFFFFFF), F32)


def _topk_mask_body(qi_ref, wt_ref, ki_ref, mask_ref, sc_ref, hi_ref, lo_ref, *, tq, tk, n_heads, top_k, seq):
    i = pl.program_id(1)
    n_kt = seq // tk
    q0 = i * tq
    n_act = (q0 + tq + tk - 1) // tk
    q_pos = q0 + lax.broadcasted_iota(I32, (1, tq), 1)
    key_iota = lax.broadcasted_iota(I32, (tk, tq), 0)
    wt = wt_ref[0]

    q_pairs = [jnp.concatenate([qi_ref[0, :, (2 * p) * IDX_DIM:(2 * p + 1) * IDX_DIM],
                                qi_ref[0, :, (2 * p + 1) * IDX_DIM:(2 * p + 2) * IDX_DIM]], axis=0)
               for p in range(n_heads // 2)]
    sub = V7X_LANES

    def score_tile(j, causal):
        for rb in range(tk // sub):
            kr = ki_ref[0, j, rb * sub:(rb + 1) * sub, :]
            pair_scores = [_dot_nt(kr, qp) for qp in q_pairs]
            acc = jnp.zeros((sub, tq), F32)
            for p, s2 in enumerate(pair_scores):
                acc = (acc + jnp.maximum(s2[:, :tq], 0.0) * wt[2 * p:2 * p + 1, :]
                       + jnp.maximum(s2[:, tq:], 0.0) * wt[2 * p + 1:2 * p + 2, :])
            acc = jnp.where(acc == 0.0, 0.0, acc)
            bits = pltpu.bitcast(acc, I32)
            key = bits ^ ((bits >> 31) & 0x7FFFFFFF)
            if causal:
                admissible = j * tk + rb * sub + key_iota[:sub] <= q_pos
                acc = jnp.where(admissible, acc, -jnp.inf)
                key = jnp.where(admissible, key, INT_MIN)
            sc_ref[j, rb * sub:(rb + 1) * sub, :] = acc
            half = sub // 2
            words = slice(rb * half, (rb + 1) * half)
            key_a, key_b = key[:half], key[half:]
            hi_ref[j, words, :] = lax.shift_right_logical(key_a, 16) | (key_b & -(1 << 16))
            lo_ref[j, words, :] = ((key_a & 0xFFFF) | jnp.left_shift(key_b, 16)) ^ FLIP_BIT15_BOTH

    def score_tile_pair(causal):
        def body(jj, carry):
            score_tile(2 * jj, causal)
            score_tile(2 * jj + 1, causal)
            return carry
        return body

    n_free = q0 // (2 * tk)
    lax.fori_loop(0, n_free, score_tile_pair(False), 0)
    lax.fori_loop(n_free, (n_act + 1) // 2, score_tile_pair(True), 0)

    def packed16(c):
        word = (c & 0xFFFF) | jnp.left_shift(c, 16)
        return pltpu.bitcast(jnp.broadcast_to(word, (8, tq)), jnp.int16)[0:1, :]

    def count16(w_ref, c):
        c16 = packed16(c)
        one = jnp.ones((), jnp.int16)
        zero = jnp.zeros((), jnp.int16)

        def body(j, acc):
            hit = jnp.where(pltpu.bitcast(w_ref[j], jnp.int16) >= c16, one, zero)
            for r in range(tk // COUNT_ROWS16):
                acc = acc + hit[r * COUNT_ROWS16:(r + 1) * COUNT_ROWS16]
            return acc
        acc = lax.fori_loop(0, n_act, body, jnp.zeros((COUNT_ROWS16, tq), jnp.int16))
        return jnp.sum(acc.astype(I32).astype(F32), axis=0, keepdims=True)

    def count(pred):
        def body(j, acc):
            hit = jnp.where(pred(sc_ref[j], j), 1.0, 0.0)
            return acc + hit.reshape(tk // COUNT_ROWS, COUNT_ROWS, tq).sum(axis=0)
        acc = lax.fori_loop(0, n_act, body, jnp.zeros((COUNT_ROWS, tq), F32))
        return jnp.sum(acc, axis=0, keepdims=True)

    kf = float(top_k)
    n_adm = (q_pos + 1).astype(F32)
    def try_value(w_ref, cand, val, n_keep, n_above):
        n = n_above + count16(w_ref, cand)
        ok = n >= kf
        return jnp.where(ok, cand, val), jnp.where(ok, n, n_keep)

    low16 = jnp.full((1, tq), -HALF16, I32)
    t_hi, n_ge = try_value(hi_ref, jnp.zeros((1, tq), I32), low16, n_adm, 0.0)
    for bit in range(14, -1, -1):
        t_hi, n_ge = try_value(hi_ref, t_hi | (1 << bit), t_hi, n_ge, 0.0)
    n_above = jnp.where(t_hi < HALF16 - 1, count16(hi_ref, jnp.minimum(t_hi + 1, HALF16 - 1)), 0.0)
    t_hi16 = packed16(t_hi)

    def keep_equal_upper(j, carry):
        same = pltpu.bitcast(hi_ref[j], jnp.int16) == t_hi16
        lo = jnp.where(same, pltpu.bitcast(lo_ref[j], jnp.int16), jnp.full((), -HALF16, jnp.int16))
        lo_ref[j] = pltpu.bitcast(lo, I32)
        return carry

    lax.fori_loop(0, n_act, keep_equal_upper, 0)
    group = 4

    def more(c):
        return jnp.logical_and(c[0] >= 0, jnp.max(c[2]) > kf)

    def bit_group(c):
        top, val, n_keep = c
        for d in range(group):
            bit = jnp.maximum(top - d, 0)
            cand = jnp.where(top - d == 15, 0, val | jnp.left_shift(jnp.int32(1), bit))
            val, n_keep = try_value(lo_ref, cand, val, n_keep, n_above)
        return top - group, val, n_keep

    _, t_lo, n_ge = lax.while_loop(more, bit_group, (jnp.int32(15), low16, n_ge))
    thr_key = jnp.left_shift(t_hi, 16) | (t_lo + HALF16)
    few = n_adm <= kf
    thr = jnp.where(few, float(jnp.finfo(F32).min), _key_to_float(thr_key))
    n_ge = jnp.where(few, n_adm, n_ge)
    ambiguous = jnp.max(n_ge) > kf
    n_bits = max(1, (seq - 1).bit_length())

    def tie_cut():
        need = kf - count(lambda s, j: s > thr)

        def step(b, x):
            cand = x | jnp.left_shift(jnp.int32(1), n_bits - 1 - b)
            cnt = count(lambda s, j: jnp.where(s == thr, j * tk + key_iota, seq) < cand)
            return jnp.where(cnt < need, cand, x)
        return lax.fori_loop(0, n_bits, step, jnp.zeros((1, tq), I32))

    cut = lax.cond(ambiguous, tie_cut, lambda: jnp.full((1, tq), seq, I32))

    def write_mask(j, carry):
        s = sc_ref[j]
        tie_pos = jnp.where(s == thr, j * tk + key_iota, seq + 1)
        sel = jnp.where(s > thr, 1, jnp.where(tie_pos <= cut, 1, 0))
        mask_ref[0, j] = sel.astype(mask_ref.dtype)
        return carry

    def write_zero(j, carry):
        mask_ref[0, j] = jnp.zeros((tk, tq), mask_ref.dtype)
        return carry

    lax.fori_loop(0, n_act, write_mask, 0)
    lax.fori_loop(n_act, n_kt, write_zero, 0)


def _topk_mask(qi, wt, ki, top_k, tk):
    b, s, iw = qi.shape
    n_heads = iw // IDX_DIM
    tq = V7X_LANES
    n_kt = s // tk
    assert n_kt % 2 == 0, "score tiles are computed in pairs"
    return pl.pallas_call(
        functools.partial(_topk_mask_body, tq=tq, tk=tk, n_heads=n_heads, top_k=top_k, seq=s),
        grid=(b, s // tq),
        in_specs=[pl.BlockSpec((1, tq, iw), lambda bi, i: (bi, i, 0)),
                  pl.BlockSpec((1, n_heads, tq), lambda bi, i: (bi, 0, i)),
                  pl.BlockSpec((1, n_kt, tk, IDX_DIM), lambda bi, i: (bi, 0, 0, 0))],
        out_specs=pl.BlockSpec((1, n_kt, tk, tq), lambda bi, i: (bi, 0, 0, i)),
        out_shape=jax.ShapeDtypeStruct((b, n_kt, tk, s), jnp.int8),
        scratch_shapes=[pltpu.VMEM((n_kt, tk, tq), F32), pltpu.VMEM((n_kt, tk // 2, tq), I32),
                        pltpu.VMEM((n_kt, tk // 2, tq), I32)],
        compiler_params=_cparams(("parallel", "arbitrary"), 48),
        name="topk_mask",
    )(qi, wt, ki)


def _attn_body(i_tab, j_tab, q_ref, k_ref, vt_ref, mask_ref, o_ref, m_ref, l_ref, acc_ref, bias_ref, s_ref, *,
               tq, tk, n_heads):
    t = pl.program_id(1)
    i = i_tab[t]
    j = j_tab[t]
    hd = ATT_HEAD_DIM
    blk = V7X_LANES

    @pl.when(j == 0)
    def _():
        m_ref[...] = jnp.full(m_ref.shape, NEG_BIG, F32)
        l_ref[...] = jnp.zeros_like(l_ref)
        acc_ref[...] = jnp.zeros_like(acc_ref)

    bias_ref[...] = jnp.where(mask_ref[0, 0].astype(F32) != 0.0, 0.0, NEG_BIG)
    lane = lax.broadcasted_iota(I32, (1, 2 * hd), 1)
    half_sel = [jnp.where(lane < hd, 1.0, 0.0).astype(BF16), jnp.where(lane >= hd, 1.0, 0.0).astype(BF16)]

    key_blocks = [slice(kb * blk, (kb + 1) * blk) for kb in range(tk // blk)]
    ones_rows = jnp.ones((BF16_SUBLANES, blk), BF16)
    m_tile = {}
    base = jnp.minimum(j, 0)

    def scores(h):
        pair = slice((h // 2) * 2 * hd, (h // 2 + 1) * 2 * hd)
        qh = q_ref[0, :, pair] * half_sel[h % 2]
        top = None
        for ks in key_blocks:
            s = _dot_nt(k_ref[0, ks, pair], qh) + bias_ref[ks, :]
            s_ref[base + h, ks, :] = s
            bm = jnp.max(s, axis=0, keepdims=True)
            top = bm if top is None else jnp.maximum(top, bm)
        m_tile[h] = top

    def update(h):
        hs = slice(h * hd, (h + 1) * hd)
        m = m_ref[h:h + 1, :]
        m_new = jnp.maximum(m, m_tile[h])
        alpha = jnp.exp(m - m_new)
        pv = None
        for ks in key_blocks:
            p = jnp.exp(s_ref[base + h, ks, :] - m_new).astype(BF16)
            lhs = jnp.concatenate([vt_ref[0, hs, ks], ones_rows], axis=0)
            d = jnp.dot(lhs, p, preferred_element_type=F32)
            pv = d if pv is None else pv + d
        l_ref[h:h + 1, :] = alpha * l_ref[h:h + 1, :] + pv[hd:hd + 1, :]
        m_ref[h:h + 1, :] = m_new
        acc_ref[hs, :] = alpha * acc_ref[hs, :] + pv[:hd, :]

    for h in range(n_heads):
        scores(h)
    for h in range(n_heads):
        update(h)

    @pl.when(j == ((i + 1) * tq - 1) // tk)
    def _():
        for h in range(n_heads):
            hs = slice(h * hd, (h + 1) * hd)
            o_ref[0, hs, :] = (acc_ref[hs, :] / l_ref[h:h + 1, :]).astype(o_ref.dtype)


def _masked_attention(q, k, vt, mask, tk):
    b, s, aw = q.shape
    n_heads = aw // ATT_HEAD_DIM
    tq = _pick_tile(s, 512, V7X_LANES)
    pairs = [(i, j) for i in range(s // tq) for j in range(((i + 1) * tq - 1) // tk + 1)]
    i_tab = jnp.asarray([p[0] for p in pairs], I32)
    j_tab = jnp.asarray([p[1] for p in pairs], I32)
    grid_spec = pltpu.PrefetchScalarGridSpec(
        num_scalar_prefetch=2,
        grid=(b, len(pairs)),
        in_specs=[pl.BlockSpec((1, tq, aw), lambda bi, t, it, jt: (bi, it[t], 0)),
                  pl.BlockSpec((1, tk, aw), lambda bi, t, it, jt: (bi, jt[t], 0)),
                  pl.BlockSpec((1, aw, tk), lambda bi, t, it, jt: (bi, 0, jt[t])),
                  pl.BlockSpec((1, 1, tk, tq), lambda bi, t, it, jt: (bi, jt[t], 0, it[t]))],
        out_specs=pl.BlockSpec((1, aw, tq), lambda bi, t, it, jt: (bi, 0, it[t])),
        scratch_shapes=[pltpu.VMEM((n_heads, tq), F32), pltpu.VMEM((n_heads, tq), F32),
                        pltpu.VMEM((aw, tq), F32), pltpu.VMEM((tk, tq), F32),
                        pltpu.VMEM((n_heads, tk, tq), F32)],
    )
    return pl.pallas_call(
        functools.partial(_attn_body, tq=tq, tk=tk, n_heads=n_heads),
        grid_spec=grid_spec,
        out_shape=jax.ShapeDtypeStruct((b, aw, s), BF16),
        compiler_params=_cparams(("parallel", "arbitrary"), 32),
        name="masked_attn",
    )(i_tab, j_tab, q, k, vt, mask)


def _outproj_body(x_ref, *refs):
    n_in = (len(refs) - 1) // 2
    acc = x_ref[...]
    for y_ref, w_ref in zip(refs[:n_in], refs[n_in:2 * n_in]):
        acc = acc + jnp.dot(y_ref[...], w_ref[...], preferred_element_type=F32)
    refs[-1][...] = acc


def _outproj(x, ys, ws):
    n, d = x.shape
    tm = _pick_tile(n, 512, 8)
    row = lambda i: (i, 0)
    fixed = lambda i: (0, 0)
    return pl.pallas_call(
        _outproj_body,
        grid=(n // tm,),
        in_specs=[pl.BlockSpec((tm, d), row)] + [pl.BlockSpec((tm, y.shape[1]), row) for y in ys]
                 + [pl.BlockSpec(w.shape, fixed) for w in ws],
        out_specs=pl.BlockSpec((tm, d), row),
        out_shape=jax.ShapeDtypeStruct((n, d), F32),
        compiler_params=_cparams(("parallel",), 32),
        name="outproj",
    )(x, *ys, *ws)


def _od_prep_body(x_ref, g_ref, w_ref, cw_ref, alog_ref, dtb_ref,
                  q_ref, k_ref, v_ref, z_ref, bg_ref, p_ref, *, tiles_per_seq, n_heads):
    tm = x_ref.shape[0]
    hist = DN_KERNEL_ROWS
    d = n_heads * DN_HEAD_DIM
    _project_rows(x_ref, g_ref, w_ref, p_ref, tiles_per_seq)
    ktaps = cw_ref.shape[0]
    for part, out_ref in enumerate((q_ref, k_ref, v_ref)):
        for h in range(n_heads):
            col = part * d + h * DN_HEAD_DIM
            conv = None
            for j in range(ktaps):
                shift = ktaps - 1 - j
                term = (p_ref[hist - shift:hist - shift + tm, col:col + DN_HEAD_DIM]
                        * cw_ref[j:j + 1, col:col + DN_HEAD_DIM])
                conv = term if conv is None else conv + term
            u = _silu(conv)
            if part < 2:
                u = u * lax.rsqrt(jnp.sum(u * u, axis=-1, keepdims=True) + NORM_EPS)
            if part == 0:
                u = u * DN_HEAD_DIM ** -0.5
            out_ref[:, h * DN_HEAD_DIM:(h + 1) * DN_HEAD_DIM] = u
    z_ref[...] = p_ref[hist:hist + tm, 3 * d:4 * d]
    slab = p_ref[hist:hist + tm, 4 * d:4 * d + V7X_LANES]
    lane = lax.broadcasted_iota(I32, slab.shape, 1)
    xs = slab + dtb_ref[...]
    softplus = jnp.maximum(xs, 0.0) + jnp.log1p(jnp.exp(-jnp.abs(xs)))
    bg_ref[...] = jnp.where(lane < n_heads, jax.nn.sigmoid(slab), -jnp.exp(alog_ref[...]) * softplus)


def _od_prep(x, gain, w_in, conv_w, a_log, dt_bias, seq):
    n, d = x.shape
    n_heads = d // DN_HEAD_DIM
    ppad = w_in.shape[1]
    tm = _pick_tile(seq, 256, 8)
    row = lambda i: (i, 0)
    fixed = lambda i: (0, 0)
    pad = lambda t: jnp.pad(t.astype(F32), (n_heads, V7X_LANES - 2 * n_heads)).reshape(1, V7X_LANES)
    outs = [jax.ShapeDtypeStruct((n, d), F32)] * 4 + [jax.ShapeDtypeStruct((n, V7X_LANES), F32)]
    return pl.pallas_call(
        functools.partial(_od_prep_body, tiles_per_seq=seq // tm, n_heads=n_heads),
        grid=(n // tm,),
        in_specs=[pl.BlockSpec((tm, d), row), pl.BlockSpec((1, d), fixed), pl.BlockSpec((d, ppad), fixed),
                  pl.BlockSpec(conv_w.shape, fixed), pl.BlockSpec((1, V7X_LANES), fixed),
                  pl.BlockSpec((1, V7X_LANES), fixed)],
        out_specs=[pl.BlockSpec((tm, d), row)] * 4 + [pl.BlockSpec((tm, V7X_LANES), row)],
        out_shape=outs,
        scratch_shapes=[pltpu.VMEM((tm + DN_KERNEL_ROWS, ppad), F32)],
        compiler_params=_cparams(("arbitrary",), 56),
        name="od_prep",
    )(x, gain.reshape(1, d), w_in, conv_w, pad(a_log), pad(dt_bias))


def _delta_body(q_ref, k_ref, v_ref, z_ref, bg_ref, bgt_ref, gain_ref, y_ref, state_ref, *, n_heads, n_chunks):
    c_len = DN_CHUNK
    hd = DN_HEAD_DIM
    heads = range(n_heads)

    @pl.when(pl.program_id(1) == 0)
    def _():
        state_ref[...] = jnp.zeros_like(state_ref)

    ri = lax.broadcasted_iota(I32, (c_len, c_len), 0)
    ci = lax.broadcasted_iota(I32, (c_len, c_len), 1)
    lower = ci <= ri
    strict = ci < ri
    eye = jnp.where(ci == ri, 1.0, 0.0)

    group = max(g for g in (1, 2, 4) if n_chunks % g == 0)

    def group_step(cg, carry):
        chunk_rows = [pl.ds(pl.multiple_of((cg * group + g) * c_len, c_len), c_len) for g in range(group)]
        bg = [bg_ref[0, chunk_rows[g], :] for g in range(group)]
        bgt = [bgt_ref[0, cg * group + g] for g in range(group)]
        units = [(g, h) for g in range(group) for h in heads]
        cols = lambda h: slice(h * hd, (h + 1) * hd)
        q = [q_ref[0, chunk_rows[g], cols(h)] for g, h in units]
        k = [k_ref[0, chunk_rows[g], cols(h)] for g, h in units]
        v = [v_ref[0, chunk_rows[g], cols(h)] for g, h in units]
        n_units = range(len(units))
        beta = [bg[g][:, h:h + 1] for g, h in units]
        gc_col = [jnp.sum(jnp.where(lower, bgt[g][n_heads + h:n_heads + h + 1, :], 0.0), axis=1, keepdims=True)
                  for g, h in units]
        gc_row = [jnp.sum(jnp.where(ri <= ci, bg[g][:, n_heads + h:n_heads + h + 1], 0.0), axis=0, keepdims=True)
                  for g, h in units]
        decay = [jnp.where(lower, jnp.exp(jnp.where(lower, gc_col[u] - gc_row[u], 0.0)), 0.0) for u in n_units]
        kb = [k[u] * beta[u] for u in n_units]
        kbf = [k[u].astype(BF16) for u in n_units]
        kq = [_dot_nt(jnp.concatenate([kb[u], q[u]], axis=0).astype(BF16), kbf[u]) for u in n_units]
        intra = [(kq[u][c_len:] * decay[u]).astype(BF16) for u in n_units]
        pw = [jnp.where(strict, -(kq[u][:c_len] * decay[u]), 0.0) for u in n_units]
        t_inv = [eye + pw[u] for u in n_units]
        span = 1
        while 2 * span < c_len:
            pwb = [p.astype(BF16) for p in pw]
            pw = [jnp.dot(pwb[u], pwb[u], preferred_element_type=F32) for u in n_units]
            t_inv = [t_inv[u] + jnp.dot(t_inv[u].astype(BF16), pw[u].astype(BF16), preferred_element_type=F32)
                     for u in n_units]
            span *= 2
        e_gc = [jnp.exp(gc_col[u]) for u in n_units]
        uw = [jnp.dot(t_inv[u].astype(BF16),
                      jnp.concatenate([v[u] * beta[u], kb[u] * e_gc[u]], axis=1).astype(BF16),
                      preferred_element_type=F32) for u in n_units]
        g_last = [gc_col[u][c_len - 1:c_len, :] for u in n_units]
        k_dec_t = [(k[u] * jnp.exp(g_last[u] - gc_col[u])).T.astype(BF16) for u in n_units]
        wq = [jnp.concatenate([uw[u][:, hd:], q[u] * e_gc[u]], axis=0).astype(BF16) for u in n_units]
        state = [state_ref[h] for h in heads]
        for g in range(group):
            us = [g * n_heads + h for h in heads]
            ws = [jnp.dot(wq[u], state[h].astype(BF16), preferred_element_type=F32) for h, u in zip(heads, us)]
            v_new = [(uw[u][:, :hd] - ws[h][:c_len]).astype(BF16) for h, u in zip(heads, us)]
            out = [ws[h][c_len:] + jnp.dot(intra[u], v_new[h], preferred_element_type=F32)
                   for h, u in zip(heads, us)]
            state = [state[h] * jnp.exp(g_last[u]) + jnp.dot(k_dec_t[u], v_new[h], preferred_element_type=F32)
                     for h, u in zip(heads, us)]
            for h in heads:
                o = out[h] * lax.rsqrt(jnp.mean(out[h] * out[h], axis=-1, keepdims=True) + NORM_EPS) * gain_ref[...]
                zh = z_ref[0, chunk_rows[g], cols(h)]
                y_ref[0, chunk_rows[g], cols(h)] = (o * _silu(zh)).astype(y_ref.dtype)
        for h in heads:
            state_ref[h] = state[h]
        return carry

    lax.fori_loop(0, n_chunks // group, group_step, 0)


def _delta_rule(q, k, v, z, bg, bgt, o_gain):
    b, s, d = q.shape
    n_heads = d // DN_HEAD_DIM
    rows = _pick_tile(s, 512, DN_CHUNK)
    n_chunks = rows // DN_CHUNK
    blk = lambda bi, i: (bi, i, 0)
    return pl.pallas_call(
        functools.partial(_delta_body, n_heads=n_heads, n_chunks=n_chunks),
        grid=(b, s // rows),
        in_specs=[pl.BlockSpec((1, rows, d), blk)] * 4 + [
            pl.BlockSpec((1, rows, V7X_LANES), blk),
            pl.BlockSpec((1, n_chunks, 2 * n_heads, DN_CHUNK), lambda bi, i: (bi, i, 0, 0)),
            pl.BlockSpec((1, DN_HEAD_DIM), lambda bi, i: (0, 0))],
        out_specs=pl.BlockSpec((1, rows, d), blk),
        out_shape=jax.ShapeDtypeStruct((b, s, d), BF16),
        scratch_shapes=[pltpu.VMEM((n_heads, DN_HEAD_DIM, DN_HEAD_DIM), F32)],
        compiler_params=_cparams(("parallel", "arbitrary"), 40),
        name="delta_rule",
    )(q, k, v, z, bg, bgt, o_gain.reshape(1, DN_HEAD_DIM))


def _rope_tables(positions):
    half = ROPE_DIM // 2
    inv_freq = ROPE_THETA ** (-jnp.arange(0, ROPE_DIM, 2, dtype=F32) / ROPE_DIM)
    ang = positions.astype(F32).reshape(-1, 1) * inv_freq
    cos, sin = jnp.cos(ang), jnp.sin(ang)
    n = ang.shape[0]
    rest = ATT_HEAD_DIM - ROPE_DIM
    zeros = lambda m: jnp.zeros((n, m), F32)
    c = jnp.concatenate([cos, cos, jnp.ones((n, rest), F32)], axis=1)
    sa = jnp.concatenate([-sin, zeros(half + rest)], axis=1)
    sb = jnp.concatenate([zeros(half), sin, zeros(rest)], axis=1)
    reps = V7X_LANES // ATT_HEAD_DIM
    return tuple(jnp.tile(t, (1, reps)) for t in (c, sa, sb))


def _pad_cols(w, mult):
    pad = (-w.shape[-1]) % mult
    return jnp.pad(w, [(0, 0)] * (w.ndim - 1) + [(0, pad)])


def kernel(x, positions, norm_gain, ffn_w_gate, ffn_w_up, ffn_w_down, ev_w_in, ev_conv_w, ev_w_out,
           od_w_in, od_conv_w, od_a_log, od_dt_bias, od_o_gain, od_w_out, final_gain):
    bsz, seq, d = x.shape
    depth = norm_gain.shape[0]
    n = bsz * seq
    assert d % (2 * V7X_LANES) == 0 and seq % V7X_LANES == 0
    top_k = min(TOPK_MAX, seq // 4)
    tk = _pick_tile(seq, 512, V7X_LANES)
    dff = ffn_w_gate.shape[-1]

    wg = ffn_w_gate.reshape(depth * 2, d, dff)
    wu = ffn_w_up.reshape(depth * 2, d, dff)
    wd = ffn_w_down.reshape(depth * 2, dff, d)
    ev_in = _pad_cols(ev_w_in, V7X_LANES).astype(BF16)
    ev_out = ev_w_out.astype(BF16)
    od_in = _pad_cols(od_w_in, V7X_LANES).astype(BF16)
    od_out = od_w_out.astype(BF16)
    tables = _rope_tables(positions)
    half_w = d // 2

    x = x.reshape(n, d)
    for layer in range(depth):
        i = layer // 2
        x = _ffn(x, norm_gain[layer, 0], wg, wu, wd, 2 * layer)
        if layer % 2 == 0:
            yc, q, k, v, qi, kiw = _ev_prep(x, norm_gain[layer, 1], ev_in[i], ev_conv_w[i], tables, seq)
            kiw3 = kiw.reshape(bsz, seq, V7X_LANES)
            ki = kiw3[:, :, :IDX_DIM].astype(BF16).reshape(bsz, seq // tk, tk, IDX_DIM)
            n_idx = half_w // IDX_DIM
            wt = jnp.swapaxes(kiw3[:, :, IDX_DIM:IDX_DIM + n_idx], 1, 2)
            to3 = lambda t: t.reshape(bsz, seq, half_w)
            mask = _topk_mask(to3(qi), wt, ki, top_k, tk)
            ya_t = _masked_attention(to3(q), to3(k), jnp.swapaxes(to3(v), 1, 2), mask, tk)
            ya = jnp.swapaxes(ya_t, 1, 2).reshape(n, half_w)
            x = _outproj(x, [yc, ya], [ev_out[i, :half_w], ev_out[i, half_w:]])
        else:
            q, k, v, z, bg = _od_prep(x, norm_gain[layer, 1], od_in[i], od_conv_w[i], od_a_log[i],
                                      od_dt_bias[i], seq)
            n_heads = d // DN_HEAD_DIM
            bg3 = bg.reshape(bsz, seq, V7X_LANES)
            bgt = jnp.swapaxes(bg3[:, :, :2 * n_heads].reshape(bsz, seq // DN_CHUNK, DN_CHUNK, 2 * n_heads),
                               2, 3)
            to3 = lambda t: t.reshape(bsz, seq, d)
            y = _delta_rule(to3(q), to3(k), to3(v), to3(z), bg3, bgt, od_o_gain[i]).reshape(n, d)
            x = _outproj(x, [y], [od_out[i]])
        x = _ffn(x, norm_gain[layer, 2], wg, wu, wd, 2 * layer + 1,
                 final_gain=final_gain if layer == depth - 1 else None)
    return x.reshape(bsz, seq, d)
```

```python
import functools
import math

import jax
import jax.numpy as jnp
from jax import lax
from jax.experimental import pallas as pl
from jax.experimental.pallas import tpu as pltpu

F32 = jnp.float32
BF16 = jnp.bfloat16
I32 = jnp.int32

NORM_EPS = 1e-6
ATT_HEAD_DIM = 64
ATT_SCALE = ATT_HEAD_DIM ** -0.5
assert ATT_SCALE == 2.0 ** round(math.log2(ATT_SCALE)), "folding the scale into bf16 q needs a power of two"
IDX_DIM = 64
DN_HEAD_DIM = 128
DN_CHUNK = 64
DN_KERNEL_ROWS = 8
TOPK_MAX = 256
ROPE_DIM = 16
ROPE_THETA = 500000.0

V7X_LANES = 128
BF16_SUBLANES = 16
V7X_VMEM_BYTES = 64 * 2**20
INT_MIN = -(2**31)
NEG_BIG = -1e30


def _cparams(semantics, vmem_mb):
    assert vmem_mb * 2**20 < V7X_VMEM_BYTES
    return pltpu.CompilerParams(dimension_semantics=semantics, vmem_limit_bytes=vmem_mb * 2**20)


def _pick_tile(n, target, mult):
    best = None
    t = mult
    while t <= min(n, target):
        if n % t == 0:
            best = t
        t += mult
    assert best is not None, (n, target, mult)
    return best


def _rmsnorm(x, gain):
    ms = jnp.mean(x * x, axis=-1, keepdims=True)
    return x * lax.rsqrt(ms + NORM_EPS) * gain


def _silu(x):
    return x * jax.nn.sigmoid(x)


def _dot_nt(a, b):
    return lax.dot_general(a, b, (((1,), (1,)), ((), ())), preferred_element_type=F32)


def _ffn_body(x_ref, g_ref, wg_ref, wu_ref, wd_ref, *rest, final):
    if final:
        fg_ref, o_ref, h_ref, acc_ref = rest
    else:
        o_ref, h_ref, acc_ref = rest
    f = pl.program_id(1)

    @pl.when(f == 0)
    def _():
        h_ref[...] = _rmsnorm(x_ref[...], g_ref[...]).astype(BF16)
        acc_ref[...] = jnp.zeros_like(acc_ref)

    h = h_ref[...]
    a = jnp.dot(h, wg_ref[...].astype(BF16), preferred_element_type=F32)
    b = jnp.dot(h, wu_ref[...].astype(BF16), preferred_element_type=F32)
    t = (_silu(a) * b).astype(BF16)
    acc_ref[...] += jnp.dot(t, wd_ref[...].astype(BF16), preferred_element_type=F32)

    @pl.when(f == pl.num_programs(1) - 1)
    def _():
        y = x_ref[...] + 0.5 * acc_ref[...]
        if final:
            y = _rmsnorm(y, fg_ref[...])
        o_ref[...] = y


def _ffn(x, gain, wg, wu, wd, widx, final_gain=None):
    n, d = x.shape
    dff = wg.shape[-1]
    tm = _pick_tile(n, 1024, 8)
    tf = _pick_tile(dff, 512, V7X_LANES)
    final = final_gain is not None
    in_specs = [
        pl.BlockSpec((tm, d), lambda i, f: (i, 0)),
        pl.BlockSpec((1, d), lambda i, f: (0, 0)),
        pl.BlockSpec((None, d, tf), lambda i, f: (widx, 0, f)),
        pl.BlockSpec((None, d, tf), lambda i, f: (widx, 0, f)),
        pl.BlockSpec((None, tf, d), lambda i, f: (widx, f, 0)),
    ]
    args = [x, gain.reshape(1, d), wg, wu, wd]
    if final:
        in_specs.append(pl.BlockSpec((1, d), lambda i, f: (0, 0)))
        args.append(final_gain.reshape(1, d))
    return pl.pallas_call(
        functools.partial(_ffn_body, final=final),
        grid=(n // tm, dff // tf),
        in_specs=in_specs,
        out_specs=pl.BlockSpec((tm, d), lambda i, f: (i, 0)),
        out_shape=jax.ShapeDtypeStruct((n, d), F32),
        scratch_shapes=[pltpu.VMEM((tm, d), BF16), pltpu.VMEM((tm, d), F32)],
        compiler_params=_cparams(("parallel", "arbitrary"), 48),
        name="ffn_final" if final else "ffn",
    )(*args)


def _project_rows(x_ref, g_ref, w_ref, p_ref, tiles_per_seq):
    i = pl.program_id(0)
    tm = x_ref.shape[0]
    hist = DN_KERNEL_ROWS

    @pl.when(i % tiles_per_seq == 0)
    def _():
        p_ref[0:hist, :] = jnp.zeros((hist, p_ref.shape[1]), F32)

    @pl.when(i % tiles_per_seq != 0)
    def _():
        p_ref[0:hist, :] = p_ref[tm:tm + hist, :]

    h = _rmsnorm(x_ref[...], g_ref[...]).astype(BF16)
    p_ref[hist:hist + tm, :] = jnp.dot(h, w_ref[...], preferred_element_type=F32)


def _rope(x, c, sa, sb):
    n = x.shape[1]
    reps = n // V7X_LANES
    if reps > 1:
        c, sa, sb = (jnp.concatenate([t] * reps, axis=1) for t in (c, sa, sb))
    half = ROPE_DIM // 2
    x_up = pltpu.roll(x, n - half, 1)
    x_dn = pltpu.roll(x, half, 1)
    return x * c + x_up * sa + x_dn * sb


def _ev_prep_body(x_ref, g_ref, w_ref, cw_ref, c_ref, sa_ref, sb_ref,
                  yc_ref, q_ref, k_ref, v_ref, qi_ref, kiw_ref, p_ref, *, tiles_per_seq, width, w_scale):
    tm = x_ref.shape[0]
    hist = DN_KERNEL_ROWS
    _project_rows(x_ref, g_ref, w_ref, p_ref, tiles_per_seq)
    wd = width

    def rows(shift, col):
        return p_ref[hist - shift:hist - shift + tm, col * wd:(col + 1) * wd]

    ktaps = cw_ref.shape[0]
    conv = None
    for j in range(ktaps):
        shift = ktaps - 1 - j
        term = (rows(shift, 1) * rows(shift, 2)) * cw_ref[j:j + 1, :]
        conv = term if conv is None else conv + term
    yc_ref[...] = (rows(0, 0) * conv).astype(BF16)

    c, sa, sb = c_ref[...], sa_ref[...], sb_ref[...]
    q_ref[...] = (_rope(rows(0, 3), c, sa, sb) * ATT_SCALE).astype(BF16)
    k_ref[...] = _rope(rows(0, 4), c, sa, sb).astype(BF16)
    v_ref[...] = rows(0, 5).astype(BF16)
    qi_ref[...] = _rope(rows(0, 6), c, sa, sb).astype(BF16)
    slab = p_ref[hist:hist + tm, 7 * wd:7 * wd + V7X_LANES]
    lane = lax.broadcasted_iota(I32, slab.shape, 1)
    kiw_ref[...] = jnp.where(lane < IDX_DIM, _rope(slab, c, sa, sb), slab * w_scale)


def _ev_prep(x, gain, w_in, conv_w, tables, seq):
    n, d = x.shape
    wd = d // 2
    ppad = w_in.shape[1]
    tm = _pick_tile(seq, 256, 8)
    row = lambda i: (i, 0)
    fixed = lambda i: (0, 0)
    n_idx_heads = wd // IDX_DIM
    outs = [jax.ShapeDtypeStruct((n, wd), BF16)] * 5 + [jax.ShapeDtypeStruct((n, V7X_LANES), F32)]
    return pl.pallas_call(
        functools.partial(_ev_prep_body, tiles_per_seq=seq // tm, width=wd,
                          w_scale=float(n_idx_heads * IDX_DIM) ** -0.5),
        grid=(n // tm,),
        in_specs=[pl.BlockSpec((tm, d), row), pl.BlockSpec((1, d), fixed), pl.BlockSpec((d, ppad), fixed),
                  pl.BlockSpec(conv_w.shape, fixed)] + [pl.BlockSpec((tm, V7X_LANES), row)] * 3,
        out_specs=[pl.BlockSpec((tm, wd), row)] * 5 + [pl.BlockSpec((tm, V7X_LANES), row)],
        out_shape=outs,
        scratch_shapes=[pltpu.VMEM((tm + DN_KERNEL_ROWS, ppad), F32)],
        compiler_params=_cparams(("arbitrary",), 48),
        name="ev_prep",
    )(x, gain.reshape(1, d), w_in, conv_w, *tables)


COUNT_ROWS = 64
COUNT_ROWS16 = 64
HALF16 = 1 << 15
FLIP_BIT15_BOTH = (HALF16 | HALF16 << 16) - (1 << 32)


def _key_to_float(key):
    return pltpu.bitcast(key ^ ((key >> 31) & 0x7FFFFFFF), F32)


def _topk_mask_body(qi_ref, wt_ref, ki_ref, mask_ref, sc_ref, hi_ref, lo_ref, *, tq, tk, n_heads, top_k, seq):
    i = pl.program_id(1)
    n_kt = seq // tk
    q0 = i * tq
    n_act = (q0 + tq + tk - 1) // tk
    q_pos = q0 + lax.broadcasted_iota(I32, (1, tq), 1)
    key_iota = lax.broadcasted_iota(I32, (tk, tq), 0)
    wt = wt_ref[0]

    q_pairs = [jnp.concatenate([qi_ref[0, :, (2 * p) * IDX_DIM:(2 * p + 1) * IDX_DIM],
                                qi_ref[0, :, (2 * p + 1) * IDX_DIM:(2 * p + 2) * IDX_DIM]], axis=0)
               for p in range(n_heads // 2)]
    sub = V7X_LANES

    def score_tile(j, causal):
        for rb in range(tk // sub):
            kr = ki_ref[0, j, rb * sub:(rb + 1) * sub, :]
            pair_scores = [_dot_nt(kr, qp) for qp in q_pairs]
            acc = jnp.zeros((sub, tq), F32)
            for p, s2 in enumerate(pair_scores):
                acc = (acc + jnp.maximum(s2[:, :tq], 0.0) * wt[2 * p:2 * p + 1, :]
                       + jnp.maximum(s2[:, tq:], 0.0) * wt[2 * p + 1:2 * p + 2, :])
            acc = jnp.where(acc == 0.0, 0.0, acc)
            bits = pltpu.bitcast(acc, I32)
            key = bits ^ ((bits >> 31) & 0x7FFFFFFF)
            if causal:
                admissible = j * tk + rb * sub + key_iota[:sub] <= q_pos
                acc = jnp.where(admissible, acc, -jnp.inf)
                key = jnp.where(admissible, key, INT_MIN)
            sc_ref[j, rb * sub:(rb + 1) * sub, :] = acc
            half = sub // 2
            words = slice(rb * half, (rb + 1) * half)
            key_a, key_b = key[:half], key[half:]
            hi_ref[j, words, :] = lax.shift_right_logical(key_a, 16) | (key_b & -(1 << 16))
            lo_ref[j, words, :] = ((key_a & 0xFFFF) | jnp.left_shift(key_b, 16)) ^ FLIP_BIT15_BOTH

    def score_tile_pair(causal):
        def body(jj, carry):
            score_tile(2 * jj, causal)
            score_tile(2 * jj + 1, causal)
            return carry
        return body

    n_free = q0 // (2 * tk)
    lax.fori_loop(0, n_free, score_tile_pair(False), 0)
    lax.fori_loop(n_free, (n_act + 1) // 2, score_tile_pair(True), 0)

    def packed16(c):
        word = (c & 0xFFFF) | jnp.left_shift(c, 16)
        return pltpu.bitcast(jnp.broadcast_to(word, (8, tq)), jnp.int16)[0:1, :]

    def count16(w_ref, c):
        c16 = packed16(c)
        one = jnp.ones((), jnp.int16)
        zero = jnp.zeros((), jnp.int16)

        def body(jj, acc):
            for j in (2 * jj, 2 * jj + 1):
                hit = jnp.where(pltpu.bitcast(w_ref[j], jnp.int16) >= c16, one, zero)
                for r in range(tk // COUNT_ROWS16):
                    acc = acc + hit[r * COUNT_ROWS16:(r + 1) * COUNT_ROWS16]
            return acc
        acc = lax.fori_loop(0, (n_act + 1) // 2, body, jnp.zeros((COUNT_ROWS16, tq), jnp.int16))
        return jnp.sum(acc.astype(I32).astype(F32), axis=0, keepdims=True)

    def count(pred):
        def body(j, acc):
            hit = jnp.where(pred(sc_ref[j], j), 1.0, 0.0)
            return acc + hit.reshape(tk // COUNT_ROWS, COUNT_ROWS, tq).sum(axis=0)
        acc = lax.fori_loop(0, n_act, body, jnp.zeros((COUNT_ROWS, tq), F32))
        return jnp.sum(acc, axis=0, keepdims=True)

    kf = float(top_k)
    n_adm = (q_pos + 1).astype(F32)
    def try_value(w_ref, cand, val, n_keep, n_above):
        n = n_above + count16(w_ref, cand)
        ok = n >= kf
        return jnp.where(ok, cand, val), jnp.where(ok, n, n_keep)

    low16 = jnp.full((1, tq), -HALF16, I32)
    t_hi, n_ge = try_value(hi_ref, jnp.zeros((1, tq), I32), low16, n_adm, 0.0)
    for bit in range(14, -1, -1):
        t_hi, n_ge = try_value(hi_ref, t_hi | (1 << bit), t_hi, n_ge, 0.0)
    n_above = jnp.where(t_hi < HALF16 - 1, count16(hi_ref, jnp.minimum(t_hi + 1, HALF16 - 1)), 0.0)
    t_hi16 = packed16(t_hi)

    def keep_equal_upper(j, carry):
        same = pltpu.bitcast(hi_ref[j], jnp.int16) == t_hi16
        lo = jnp.where(same, pltpu.bitcast(lo_ref[j], jnp.int16), jnp.full((), -HALF16, jnp.int16))
        lo_ref[j] = pltpu.bitcast(lo, I32)
        return carry

    lax.fori_loop(0, n_act, keep_equal_upper, 0)
    group = 4

    def more(c):
        return jnp.logical_and(c[0] >= 0, jnp.max(c[2]) > kf)

    def bit_group(c):
        top, val, n_keep = c
        for d in range(group):
            bit = jnp.maximum(top - d, 0)
            cand = jnp.where(top - d == 15, 0, val | jnp.left_shift(jnp.int32(1), bit))
            val, n_keep = try_value(lo_ref, cand, val, n_keep, n_above)
        return top - group, val, n_keep

    _, t_lo, n_ge = lax.while_loop(more, bit_group, (jnp.int32(15), low16, n_ge))
    thr_key = jnp.left_shift(t_hi, 16) | (t_lo + HALF16)
    few = n_adm <= kf
    thr = jnp.where(few, float(jnp.finfo(F32).min), _key_to_float(thr_key))
    n_ge = jnp.where(few, n_adm, n_ge)
    ambiguous = jnp.max(n_ge) > kf
    n_bits = max(1, (seq - 1).bit_length())

    def tie_cut():
        need = kf - count(lambda s, j: s > thr)

        def step(b, x):
            cand = x | jnp.left_shift(jnp.int32(1), n_bits - 1 - b)
            cnt = count(lambda s, j: jnp.where(s == thr, j * tk + key_iota, seq) < cand)
            return jnp.where(cnt < need, cand, x)
        return lax.fori_loop(0, n_bits, step, jnp.zeros((1, tq), I32))

    def write_with_ties():
        cut = tie_cut()

        def write_mask(j, carry):
            s = sc_ref[j]
            tie_pos = jnp.where(s == thr, j * tk + key_iota, seq + 1)
            sel = jnp.where(s > thr, 1, jnp.where(tie_pos <= cut, 1, 0))
            mask_ref[0, j] = sel.astype(mask_ref.dtype)
            return carry
        return lax.fori_loop(0, n_act, write_mask, 0)

    def write_plain():
        def write_mask(j, carry):
            mask_ref[0, j] = jnp.where(sc_ref[j] >= thr, 1, 0).astype(mask_ref.dtype)
            return carry
        return lax.fori_loop(0, n_act, write_mask, 0)

    def write_zero(j, carry):
        mask_ref[0, j] = jnp.zeros((tk, tq), mask_ref.dtype)
        return carry

    lax.cond(ambiguous, write_with_ties, write_plain)
    lax.fori_loop(n_act, n_kt, write_zero, 0)


def _topk_mask(qi, wt, ki, top_k, tk):
    b, s, iw = qi.shape
    n_heads = iw // IDX_DIM
    tq = V7X_LANES
    n_kt = s // tk
    assert n_kt % 2 == 0, "score tiles are computed in pairs"
    return pl.pallas_call(
        functools.partial(_topk_mask_body, tq=tq, tk=tk, n_heads=n_heads, top_k=top_k, seq=s),
        grid=(b, s // tq),
        in_specs=[pl.BlockSpec((1, tq, iw), lambda bi, i: (bi, i, 0)),
                  pl.BlockSpec((1, n_heads, tq), lambda bi, i: (bi, 0, i)),
                  pl.BlockSpec((1, n_kt, tk, IDX_DIM), lambda bi, i: (bi, 0, 0, 0))],
        out_specs=pl.BlockSpec((1, n_kt, tk, tq), lambda bi, i: (bi, 0, 0, i)),
        out_shape=jax.ShapeDtypeStruct((b, n_kt, tk, s), jnp.int8),
        scratch_shapes=[pltpu.VMEM((n_kt, tk, tq), F32), pltpu.VMEM((n_kt, tk // 2, tq), I32),
                        pltpu.VMEM((n_kt, tk // 2, tq), I32)],
        compiler_params=_cparams(("parallel", "arbitrary"), 48),
        name="topk_mask",
    )(qi, wt, ki)


def _attn_body(i_tab, j_tab, q_ref, k_ref, vt_ref, mask_ref, o_ref, m_ref, l_ref, acc_ref, bias_ref, s_ref, *,
               tq, tk, n_heads):
    t = pl.program_id(1)
    i = i_tab[t]
    j = j_tab[t]
    hd = ATT_HEAD_DIM
    blk = V7X_LANES

    @pl.when(j == 0)
    def _():
        m_ref[...] = jnp.full(m_ref.shape, NEG_BIG, F32)
        l_ref[...] = jnp.zeros_like(l_ref)
        acc_ref[...] = jnp.zeros_like(acc_ref)

    bias_ref[...] = jnp.where(mask_ref[0, 0].astype(F32) != 0.0, 0.0, NEG_BIG)
    lane = lax.broadcasted_iota(I32, (1, 2 * hd), 1)
    half_sel = [jnp.where(lane < hd, 1.0, 0.0).astype(BF16), jnp.where(lane >= hd, 1.0, 0.0).astype(BF16)]

    key_blocks = [slice(kb * blk, (kb + 1) * blk) for kb in range(tk // blk)]
    ones_rows = jnp.ones((BF16_SUBLANES, blk), BF16)
    m_tile = {}
    base = jnp.minimum(j, 0)

    def scores(h):
        pair = slice((h // 2) * 2 * hd, (h // 2 + 1) * 2 * hd)
        qh = q_ref[0, :, pair] * half_sel[h % 2]
        top = None
        for ks in key_blocks:
            s = _dot_nt(k_ref[0, ks, pair], qh) + bias_ref[ks, :]
            s_ref[base + h, ks, :] = s
            bm = jnp.max(s, axis=0, keepdims=True)
            top = bm if top is None else jnp.maximum(top, bm)
        m_tile[h] = top

    def update(h):
        hs = slice(h * hd, (h + 1) * hd)
        m = m_ref[h:h + 1, :]
        m_new = jnp.maximum(m, m_tile[h])
        alpha = jnp.exp(m - m_new)
        pv = None
        for ks in key_blocks:
            p = jnp.exp(s_ref[base + h, ks, :] - m_new).astype(BF16)
            lhs = jnp.concatenate([vt_ref[0, hs, ks], ones_rows], axis=0)
            d = jnp.dot(lhs, p, preferred_element_type=F32)
            pv = d if pv is None else pv + d
        l_ref[h:h + 1, :] = alpha * l_ref[h:h + 1, :] + pv[hd:hd + 1, :]
        m_ref[h:h + 1, :] = m_new
        acc_ref[hs, :] = alpha * acc_ref[hs, :] + pv[:hd, :]

    for h in range(n_heads):
        scores(h)
    for h in range(n_heads):
        update(h)

    @pl.when(j == ((i + 1) * tq - 1) // tk)
    def _():
        for h in range(n_heads):
            hs = slice(h * hd, (h + 1) * hd)
            o_ref[0, hs, :] = (acc_ref[hs, :] / l_ref[h:h + 1, :]).astype(o_ref.dtype)


def _masked_attention(q, k, vt, mask, tk):
    b, s, aw = q.shape
    n_heads = aw // ATT_HEAD_DIM
    tq = _pick_tile(s, 512, V7X_LANES)
    pairs = [(i, j) for i in range(s // tq) for j in range(((i + 1) * tq - 1) // tk + 1)]
    i_tab = jnp.asarray([p[0] for p in pairs], I32)
    j_tab = jnp.asarray([p[1] for p in pairs], I32)
    grid_spec = pltpu.PrefetchScalarGridSpec(
        num_scalar_prefetch=2,
        grid=(b, len(pairs)),
        in_specs=[pl.BlockSpec((1, tq, aw), lambda bi, t, it, jt: (bi, it[t], 0)),
                  pl.BlockSpec((1, tk, aw), lambda bi, t, it, jt: (bi, jt[t], 0)),
                  pl.BlockSpec((1, aw, tk), lambda bi, t, it, jt: (bi, 0, jt[t])),
                  pl.BlockSpec((1, 1, tk, tq), lambda bi, t, it, jt: (bi, jt[t], 0, it[t]))],
        out_specs=pl.BlockSpec((1, aw, tq), lambda bi, t, it, jt: (bi, 0, it[t])),
        scratch_shapes=[pltpu.VMEM((n_heads, tq), F32), pltpu.VMEM((n_heads, tq), F32),
                        pltpu.VMEM((aw, tq), F32), pltpu.VMEM((tk, tq), F32),
                        pltpu.VMEM((n_heads, tk, tq), F32)],
    )
    return pl.pallas_call(
        functools.partial(_attn_body, tq=tq, tk=tk, n_heads=n_heads),
        grid_spec=grid_spec,
        out_shape=jax.ShapeDtypeStruct((b, aw, s), BF16),
        compiler_params=_cparams(("parallel", "arbitrary"), 32),
        name="masked_attn",
    )(i_tab, j_tab, q, k, vt, mask)


def _outproj_body(x_ref, *refs):
    n_in = (len(refs) - 1) // 2
    acc = x_ref[...]
    for y_ref, w_ref in zip(refs[:n_in], refs[n_in:2 * n_in]):
        acc = acc + jnp.dot(y_ref[...], w_ref[...], preferred_element_type=F32)
    refs[-1][...] = acc


def _outproj(x, ys, ws):
    n, d = x.shape
    tm = _pick_tile(n, 512, 8)
    row = lambda i: (i, 0)
    fixed = lambda i: (0, 0)
    return pl.pallas_call(
        _outproj_body,
        grid=(n // tm,),
        in_specs=[pl.BlockSpec((tm, d), row)] + [pl.BlockSpec((tm, y.shape[1]), row) for y in ys]
                 + [pl.BlockSpec(w.shape, fixed) for w in ws],
        out_specs=pl.BlockSpec((tm, d), row),
        out_shape=jax.ShapeDtypeStruct((n, d), F32),
        compiler_params=_cparams(("parallel",), 32),
        name="outproj",
    )(x, *ys, *ws)


def _od_prep_body(x_ref, g_ref, w_ref, cw_ref, alog_ref, dtb_ref,
                  q_ref, k_ref, v_ref, z_ref, bg_ref, p_ref, *, tiles_per_seq, n_heads):
    tm = x_ref.shape[0]
    hist = DN_KERNEL_ROWS
    d = n_heads * DN_HEAD_DIM
    _project_rows(x_ref, g_ref, w_ref, p_ref, tiles_per_seq)
    ktaps = cw_ref.shape[0]
    for part, out_ref in enumerate((q_ref, k_ref, v_ref)):
        for h in range(n_heads):
            col = part * d + h * DN_HEAD_DIM
            conv = None
            for j in range(ktaps):
                shift = ktaps - 1 - j
                term = (p_ref[hist - shift:hist - shift + tm, col:col + DN_HEAD_DIM]
                        * cw_ref[j:j + 1, col:col + DN_HEAD_DIM])
                conv = term if conv is None else conv + term
            u = _silu(conv)
            if part < 2:
                u = u * lax.rsqrt(jnp.sum(u * u, axis=-1, keepdims=True) + NORM_EPS)
            if part == 0:
                u = u * DN_HEAD_DIM ** -0.5
            out_ref[:, h * DN_HEAD_DIM:(h + 1) * DN_HEAD_DIM] = u
    z_ref[...] = p_ref[hist:hist + tm, 3 * d:4 * d]
    slab = p_ref[hist:hist + tm, 4 * d:4 * d + V7X_LANES]
    lane = lax.broadcasted_iota(I32, slab.shape, 1)
    xs = slab + dtb_ref[...]
    softplus = jnp.maximum(xs, 0.0) + jnp.log1p(jnp.exp(-jnp.abs(xs)))
    bg_ref[...] = jnp.where(lane < n_heads, jax.nn.sigmoid(slab), -jnp.exp(alog_ref[...]) * softplus)


def _od_prep(x, gain, w_in, conv_w, a_log, dt_bias, seq):
    n, d = x.shape
    n_heads = d // DN_HEAD_DIM
    ppad = w_in.shape[1]
    tm = _pick_tile(seq, 256, 8)
    row = lambda i: (i, 0)
    fixed = lambda i: (0, 0)
    pad = lambda t: jnp.pad(t.astype(F32), (n_heads, V7X_LANES - 2 * n_heads)).reshape(1, V7X_LANES)
    outs = [jax.ShapeDtypeStruct((n, d), F32)] * 4 + [jax.ShapeDtypeStruct((n, V7X_LANES), F32)]
    return pl.pallas_call(
        functools.partial(_od_prep_body, tiles_per_seq=seq // tm, n_heads=n_heads),
        grid=(n // tm,),
        in_specs=[pl.BlockSpec((tm, d), row), pl.BlockSpec((1, d), fixed), pl.BlockSpec((d, ppad), fixed),
                  pl.BlockSpec(conv_w.shape, fixed), pl.BlockSpec((1, V7X_LANES), fixed),
                  pl.BlockSpec((1, V7X_LANES), fixed)],
        out_specs=[pl.BlockSpec((tm, d), row)] * 4 + [pl.BlockSpec((tm, V7X_LANES), row)],
        out_shape=outs,
        scratch_shapes=[pltpu.VMEM((tm + DN_KERNEL_ROWS, ppad), F32)],
        compiler_params=_cparams(("arbitrary",), 56),
        name="od_prep",
    )(x, gain.reshape(1, d), w_in, conv_w, pad(a_log), pad(dt_bias))


def _delta_body(q_ref, k_ref, v_ref, z_ref, bg_ref, bgt_ref, gain_ref, y_ref, state_ref, *, n_heads, n_chunks):
    c_len = DN_CHUNK
    hd = DN_HEAD_DIM
    heads = range(n_heads)

    @pl.when(pl.program_id(1) == 0)
    def _():
        state_ref[...] = jnp.zeros_like(state_ref)

    ri = lax.broadcasted_iota(I32, (c_len, c_len), 0)
    ci = lax.broadcasted_iota(I32, (c_len, c_len), 1)
    lower = ci <= ri
    strict = ci < ri
    eye = jnp.where(ci == ri, 1.0, 0.0)

    group = max(g for g in (1, 2, 4) if n_chunks % g == 0)

    def group_step(cg, carry):
        chunk_rows = [pl.ds(pl.multiple_of((cg * group + g) * c_len, c_len), c_len) for g in range(group)]
        bg = [bg_ref[0, chunk_rows[g], :] for g in range(group)]
        bgt = [bgt_ref[0, cg * group + g] for g in range(group)]
        units = [(g, h) for g in range(group) for h in heads]
        cols = lambda h: slice(h * hd, (h + 1) * hd)
        q = [q_ref[0, chunk_rows[g], cols(h)] for g, h in units]
        k = [k_ref[0, chunk_rows[g], cols(h)] for g, h in units]
        v = [v_ref[0, chunk_rows[g], cols(h)] for g, h in units]
        n_units = range(len(units))
        beta = [bg[g][:, h:h + 1] for g, h in units]
        gc_col = [jnp.sum(jnp.where(lower, bgt[g][n_heads + h:n_heads + h + 1, :], 0.0), axis=1, keepdims=True)
                  for g, h in units]
        gc_row = [jnp.sum(jnp.where(ri <= ci, bg[g][:, n_heads + h:n_heads + h + 1], 0.0), axis=0, keepdims=True)
                  for g, h in units]
        decay = [jnp.where(lower, jnp.exp(jnp.where(lower, gc_col[u] - gc_row[u], 0.0)), 0.0) for u in n_units]
        kb = [k[u] * beta[u] for u in n_units]
        kbf = [k[u].astype(BF16) for u in n_units]
        kq = [_dot_nt(jnp.concatenate([kb[u], q[u]], axis=0).astype(BF16), kbf[u]) for u in n_units]
        intra = [(kq[u][c_len:] * decay[u]).astype(BF16) for u in n_units]
        pw = [jnp.where(strict, -(kq[u][:c_len] * decay[u]), 0.0) for u in n_units]
        t_inv = [eye + pw[u] for u in n_units]
        span = 1
        while 2 * span < c_len:
            pwb = [p.astype(BF16) for p in pw]
            pw = [jnp.dot(pwb[u], pwb[u], preferred_element_type=F32) for u in n_units]
            t_inv = [t_inv[u] + jnp.dot(t_inv[u].astype(BF16), pw[u].astype(BF16), preferred_element_type=F32)
                     for u in n_units]
            span *= 2
        e_gc = [jnp.exp(gc_col[u]) for u in n_units]
        uw = [jnp.dot(t_inv[u].astype(BF16),
                      jnp.concatenate([v[u] * beta[u], kb[u] * e_gc[u]], axis=1).astype(BF16),
                      preferred_element_type=F32) for u in n_units]
        g_last = [gc_col[u][c_len - 1:c_len, :] for u in n_units]
        k_dec_t = [(k[u] * jnp.exp(g_last[u] - gc_col[u])).T.astype(BF16) for u in n_units]
        wq = [jnp.concatenate([uw[u][:, hd:], q[u] * e_gc[u]], axis=0).astype(BF16) for u in n_units]
        state = [state_ref[h] for h in heads]
        for g in range(group):
            us = [g * n_heads + h for h in heads]
            ws = [jnp.dot(wq[u], state[h].astype(BF16), preferred_element_type=F32) for h, u in zip(heads, us)]
            v_new = [(uw[u][:, :hd] - ws[h][:c_len]).astype(BF16) for h, u in zip(heads, us)]
            out = [ws[h][c_len:] + jnp.dot(intra[u], v_new[h], preferred_element_type=F32)
                   for h, u in zip(heads, us)]
            state = [state[h] * jnp.exp(g_last[u]) + jnp.dot(k_dec_t[u], v_new[h], preferred_element_type=F32)
                     for h, u in zip(heads, us)]
            for h in heads:
                o = out[h] * lax.rsqrt(jnp.mean(out[h] * out[h], axis=-1, keepdims=True) + NORM_EPS) * gain_ref[...]
                zh = z_ref[0, chunk_rows[g], cols(h)]
                y_ref[0, chunk_rows[g], cols(h)] = (o * _silu(zh)).astype(y_ref.dtype)
        for h in heads:
            state_ref[h] = state[h]
        return carry

    lax.fori_loop(0, n_chunks // group, group_step, 0)


def _delta_rule(q, k, v, z, bg, bgt, o_gain):
    b, s, d = q.shape
    n_heads = d // DN_HEAD_DIM
    rows = _pick_tile(s, 512, DN_CHUNK)
    n_chunks = rows // DN_CHUNK
    blk = lambda bi, i: (bi, i, 0)
    return pl.pallas_call(
        functools.partial(_delta_body, n_heads=n_heads, n_chunks=n_chunks),
        grid=(b, s // rows),
        in_specs=[pl.BlockSpec((1, rows, d), blk)] * 4 + [
            pl.BlockSpec((1, rows, V7X_LANES), blk),
            pl.BlockSpec((1, n_chunks, 2 * n_heads, DN_CHUNK), lambda bi, i: (bi, i, 0, 0)),
            pl.BlockSpec((1, DN_HEAD_DIM), lambda bi, i: (0, 0))],
        out_specs=pl.BlockSpec((1, rows, d), blk),
        out_shape=jax.ShapeDtypeStruct((b, s, d), BF16),
        scratch_shapes=[pltpu.VMEM((n_heads, DN_HEAD_DIM, DN_HEAD_DIM), F32)],
        compiler_params=_cparams(("parallel", "arbitrary"), 40),
        name="delta_rule",
    )(q, k, v, z, bg, bgt, o_gain.reshape(1, DN_HEAD_DIM))


def _rope_tables(positions):
    half = ROPE_DIM // 2
    inv_freq = ROPE_THETA ** (-jnp.arange(0, ROPE_DIM, 2, dtype=F32) / ROPE_DIM)
    ang = positions.astype(F32).reshape(-1, 1) * inv_freq
    cos, sin = jnp.cos(ang), jnp.sin(ang)
    n = ang.shape[0]
    rest = ATT_HEAD_DIM - ROPE_DIM
    zeros = lambda m: jnp.zeros((n, m), F32)
    c = jnp.concatenate([cos, cos, jnp.ones((n, rest), F32)], axis=1)
    sa = jnp.concatenate([-sin, zeros(half + rest)], axis=1)
    sb = jnp.concatenate([zeros(half), sin, zeros(rest)], axis=1)
    reps = V7X_LANES // ATT_HEAD_DIM
    return tuple(jnp.tile(t, (1, reps)) for t in (c, sa, sb))


def _pad_cols(w, mult):
    pad = (-w.shape[-1]) % mult
    return jnp.pad(w, [(0, 0)] * (w.ndim - 1) + [(0, pad)])


def kernel(x, positions, norm_gain, ffn_w_gate, ffn_w_up, ffn_w_down, ev_w_in, ev_conv_w, ev_w_out,
           od_w_in, od_conv_w, od_a_log, od_dt_bias, od_o_gain, od_w_out, final_gain):
    bsz, seq, d = x.shape
    depth = norm_gain.shape[0]
    n = bsz * seq
    assert d % (2 * V7X_LANES) == 0 and seq % V7X_LANES == 0
    top_k = min(TOPK_MAX, seq // 4)
    tk = _pick_tile(seq, 512, V7X_LANES)
    dff = ffn_w_gate.shape[-1]

    wg = ffn_w_gate.reshape(depth * 2, d, dff)
    wu = ffn_w_up.reshape(depth * 2, d, dff)
    wd = ffn_w_down.reshape(depth * 2, dff, d)
    ev_in = _pad_cols(ev_w_in, V7X_LANES).astype(BF16)
    ev_out = ev_w_out.astype(BF16)
    od_in = _pad_cols(od_w_in, V7X_LANES).astype(BF16)
    od_out = od_w_out.astype(BF16)
    tables = _rope_tables(positions)
    half_w = d // 2

    x = x.reshape(n, d)
    for layer in range(depth):
        i = layer // 2
        x = _ffn(x, norm_gain[layer, 0], wg, wu, wd, 2 * layer)
        if layer % 2 == 0:
            yc, q, k, v, qi, kiw = _ev_prep(x, norm_gain[layer, 1], ev_in[i], ev_conv_w[i], tables, seq)
            kiw3 = kiw.reshape(bsz, seq, V7X_LANES)
            ki = kiw3[:, :, :IDX_DIM].astype(BF16).reshape(bsz, seq // tk, tk, IDX_DIM)
            n_idx = half_w // IDX_DIM
            wt = jnp.swapaxes(kiw3[:, :, IDX_DIM:IDX_DIM + n_idx], 1, 2)
            to3 = lambda t: t.reshape(bsz, seq, half_w)
            mask = _topk_mask(to3(qi), wt, ki, top_k, tk)
            ya_t = _masked_attention(to3(q), to3(k), jnp.swapaxes(to3(v), 1, 2), mask, tk)
            ya = jnp.swapaxes(ya_t, 1, 2).reshape(n, half_w)
            x = _outproj(x, [yc, ya], [ev_out[i, :half_w], ev_out[i, half_w:]])
        else:
            q, k, v, z, bg = _od_prep(x, norm_gain[layer, 1], od_in[i], od_conv_w[i], od_a_log[i],
                                      od_dt_bias[i], seq)
            n_heads = d // DN_HEAD_DIM
            bg3 = bg.reshape(bsz, seq, V7X_LANES)
            bgt = jnp.swapaxes(bg3[:, :, :2 * n_heads].reshape(bsz, seq // DN_CHUNK, DN_CHUNK, 2 * n_heads),
                               2, 3)
            to3 = lambda t: t.reshape(bsz, seq, d)
            y = _delta_rule(to3(q), to3(k), to3(v), to3(z), bg3, bgt, od_o_gain[i]).reshape(n, d)
            x = _outproj(x, [y], [od_out[i]])
        x = _ffn(x, norm_gain[layer, 2], wg, wu, wd, 2 * layer + 1,
                 final_gain=final_gain if layer == depth - 1 else None)
    return x.reshape(bsz, seq, d)
```

```python
import functools
import math

import jax
import jax.numpy as jnp
from jax import lax
from jax.experimental import pallas as pl
from jax.experimental.pallas import tpu as pltpu

F32 = jnp.float32
BF16 = jnp.bfloat16
I32 = jnp.int32

NORM_EPS = 1e-6
ATT_HEAD_DIM = 64
ATT_SCALE = ATT_HEAD_DIM ** -0.5
assert ATT_SCALE == 2.0 ** round(math.log2(ATT_SCALE)), "folding the scale into bf16 q needs a power of two"
IDX_DIM = 64
DN_HEAD_DIM = 128
DN_CHUNK = 64
DN_KERNEL_ROWS = 8
TOPK_MAX = 256
ROPE_DIM = 16
ROPE_THETA = 500000.0

V7X_LANES = 128
BF16_SUBLANES = 16
V7X_VMEM_BYTES = 64 * 2**20
INT_MIN = -(2**31)
NEG_BIG = -1e30


VMEM_MIB = {"ffn": 48, "ev_prep": 48, "topk_mask": 48, "masked_attn": 32, "outproj": 32, "od_prep": 56,
            "delta_rule": 40}


def _cparams(kernel_name, semantics):
    vmem_bytes = VMEM_MIB[kernel_name] * 2**20
    assert vmem_bytes < V7X_VMEM_BYTES
    return pltpu.CompilerParams(dimension_semantics=semantics, vmem_limit_bytes=vmem_bytes)


def _pick_tile(n, target, mult):
    best = None
    t = mult
    while t <= min(n, target):
        if n % t == 0:
            best = t
        t += mult
    assert best is not None, (n, target, mult)
    return best


def _rmsnorm(x, gain):
    ms = jnp.mean(x * x, axis=-1, keepdims=True)
    return x * lax.rsqrt(ms + NORM_EPS) * gain


def _silu(x):
    return x * jax.nn.sigmoid(x)


def _dot_nt(a, b):
    return lax.dot_general(a, b, (((1,), (1,)), ((), ())), preferred_element_type=F32)


def _ffn_body(x_ref, g_ref, wg_ref, wu_ref, wd_ref, *rest, final):
    if final:
        fg_ref, o_ref, h_ref, acc_ref = rest
    else:
        o_ref, h_ref, acc_ref = rest
    f = pl.program_id(1)

    @pl.when(f == 0)
    def _():
        h_ref[...] = _rmsnorm(x_ref[...], g_ref[...]).astype(BF16)
        acc_ref[...] = jnp.zeros_like(acc_ref)

    h = h_ref[...]
    a = jnp.dot(h, wg_ref[...].astype(BF16), preferred_element_type=F32)
    b = jnp.dot(h, wu_ref[...].astype(BF16), preferred_element_type=F32)
    t = (_silu(a) * b).astype(BF16)
    acc_ref[...] += jnp.dot(t, wd_ref[...].astype(BF16), preferred_element_type=F32)

    @pl.when(f == pl.num_programs(1) - 1)
    def _():
        y = x_ref[...] + 0.5 * acc_ref[...]
        if final:
            y = _rmsnorm(y, fg_ref[...])
        o_ref[...] = y


def _ffn(x, gain, wg, wu, wd, widx, final_gain=None):
    n, d = x.shape
    dff = wg.shape[-1]
    tm = _pick_tile(n, 1024, 8)
    tf = _pick_tile(dff, 512, V7X_LANES)
    final = final_gain is not None
    in_specs = [
        pl.BlockSpec((tm, d), lambda i, f: (i, 0)),
        pl.BlockSpec((1, d), lambda i, f: (0, 0)),
        pl.BlockSpec((None, d, tf), lambda i, f: (widx, 0, f)),
        pl.BlockSpec((None, d, tf), lambda i, f: (widx, 0, f)),
        pl.BlockSpec((None, tf, d), lambda i, f: (widx, f, 0)),
    ]
    args = [x, gain.reshape(1, d), wg, wu, wd]
    if final:
        in_specs.append(pl.BlockSpec((1, d), lambda i, f: (0, 0)))
        args.append(final_gain.reshape(1, d))
    return pl.pallas_call(
        functools.partial(_ffn_body, final=final),
        grid=(n // tm, dff // tf),
        in_specs=in_specs,
        out_specs=pl.BlockSpec((tm, d), lambda i, f: (i, 0)),
        out_shape=jax.ShapeDtypeStruct((n, d), F32),
        scratch_shapes=[pltpu.VMEM((tm, d), BF16), pltpu.VMEM((tm, d), F32)],
        compiler_params=_cparams("ffn", ("parallel", "arbitrary")),
        name="ffn_final" if final else "ffn",
    )(*args)


def _project_rows(x_ref, g_ref, w_ref, p_ref, tiles_per_seq):
    i = pl.program_id(0)
    tm = x_ref.shape[0]
    hist = DN_KERNEL_ROWS

    @pl.when(i % tiles_per_seq == 0)
    def _():
        p_ref[0:hist, :] = jnp.zeros((hist, p_ref.shape[1]), F32)

    @pl.when(i % tiles_per_seq != 0)
    def _():
        p_ref[0:hist, :] = p_ref[tm:tm + hist, :]

    h = _rmsnorm(x_ref[...], g_ref[...]).astype(BF16)
    p_ref[hist:hist + tm, :] = jnp.dot(h, w_ref[...], preferred_element_type=F32)


def _rope(x, c, sa, sb):
    n = x.shape[1]
    reps = n // V7X_LANES
    if reps > 1:
        c, sa, sb = (jnp.concatenate([t] * reps, axis=1) for t in (c, sa, sb))
    half = ROPE_DIM // 2
    x_up = pltpu.roll(x, n - half, 1)
    x_dn = pltpu.roll(x, half, 1)
    return x * c + x_up * sa + x_dn * sb


def _ev_prep_body(x_ref, g_ref, w_ref, cw_ref, c_ref, sa_ref, sb_ref,
                  yc_ref, q_ref, k_ref, v_ref, qi_ref, kiw_ref, p_ref, *, tiles_per_seq, width, w_scale):
    tm = x_ref.shape[0]
    hist = DN_KERNEL_ROWS
    _project_rows(x_ref, g_ref, w_ref, p_ref, tiles_per_seq)
    wd = width

    def rows(shift, col):
        return p_ref[hist - shift:hist - shift + tm, col * wd:(col + 1) * wd]

    ktaps = cw_ref.shape[0]
    u = p_ref[0:hist + tm, wd:2 * wd] * p_ref[0:hist + tm, 2 * wd:3 * wd]
    conv = None
    for j in range(ktaps):
        shift = ktaps - 1 - j
        delayed = u if shift == 0 else pltpu.roll(u, shift, 0)
        term = delayed[hist:hist + tm] * cw_ref[j:j + 1, :]
        conv = term if conv is None else conv + term
    yc_ref[...] = (rows(0, 0) * conv).astype(BF16)

    c, sa, sb = c_ref[...], sa_ref[...], sb_ref[...]
    q_ref[...] = (_rope(rows(0, 3), c, sa, sb) * ATT_SCALE).astype(BF16)
    k_ref[...] = _rope(rows(0, 4), c, sa, sb).astype(BF16)
    v_ref[...] = rows(0, 5).astype(BF16)
    qi_ref[...] = _rope(rows(0, 6), c, sa, sb).astype(BF16)
    slab = p_ref[hist:hist + tm, 7 * wd:7 * wd + V7X_LANES]
    lane = lax.broadcasted_iota(I32, slab.shape, 1)
    kiw_ref[...] = jnp.where(lane < IDX_DIM, _rope(slab, c, sa, sb), slab * w_scale)


def _ev_prep(x, gain, w_in, conv_w, tables, seq):
    n, d = x.shape
    wd = d // 2
    ppad = w_in.shape[1]
    tm = _pick_tile(seq, 256, 8)
    row = lambda i: (i, 0)
    fixed = lambda i: (0, 0)
    n_idx_heads = wd // IDX_DIM
    outs = [jax.ShapeDtypeStruct((n, wd), BF16)] * 5 + [jax.ShapeDtypeStruct((n, V7X_LANES), F32)]
    return pl.pallas_call(
        functools.partial(_ev_prep_body, tiles_per_seq=seq // tm, width=wd,
                          w_scale=float(n_idx_heads * IDX_DIM) ** -0.5),
        grid=(n // tm,),
        in_specs=[pl.BlockSpec((tm, d), row), pl.BlockSpec((1, d), fixed), pl.BlockSpec((d, ppad), fixed),
                  pl.BlockSpec(conv_w.shape, fixed)] + [pl.BlockSpec((tm, V7X_LANES), row)] * 3,
        out_specs=[pl.BlockSpec((tm, wd), row)] * 5 + [pl.BlockSpec((tm, V7X_LANES), row)],
        out_shape=outs,
        scratch_shapes=[pltpu.VMEM((tm + DN_KERNEL_ROWS, ppad), F32)],
        compiler_params=_cparams("ev_prep", ("arbitrary",)),
        name="ev_prep",
    )(x, gain.reshape(1, d), w_in, conv_w, *tables)


COUNT_ROWS = 64
COUNT_ROWS16 = 64
HALF16 = 1 << 15
FLIP_BIT15_BOTH = (HALF16 | HALF16 << 16) - (1 << 32)


def _key_to_float(key):
    return pltpu.bitcast(key ^ ((key >> 31) & 0x7FFFFFFF), F32)


def _topk_mask_body(qi_ref, wt_ref, ki_ref, mask_ref, sc_ref, hi_ref, lo_ref, *, tq, tk, n_heads, top_k, seq):
    i = pl.program_id(1)
    n_kt = seq // tk
    q0 = i * tq
    n_act = (q0 + tq + tk - 1) // tk
    q_pos = q0 + lax.broadcasted_iota(I32, (1, tq), 1)
    key_iota = lax.broadcasted_iota(I32, (tk, tq), 0)
    wt = wt_ref[0]

    q_pairs = [jnp.concatenate([qi_ref[0, :, (2 * p) * IDX_DIM:(2 * p + 1) * IDX_DIM],
                                qi_ref[0, :, (2 * p + 1) * IDX_DIM:(2 * p + 2) * IDX_DIM]], axis=0)
               for p in range(n_heads // 2)]
    sub = V7X_LANES

    def score_tile(j, causal):
        for rb in range(tk // sub):
            kr = ki_ref[0, j, rb * sub:(rb + 1) * sub, :]
            pair_scores = [_dot_nt(kr, qp) for qp in q_pairs]
            acc = jnp.zeros((sub, tq), F32)
            for p, s2 in enumerate(pair_scores):
                acc = (acc + jnp.maximum(s2[:, :tq], 0.0) * wt[2 * p:2 * p + 1, :]
                       + jnp.maximum(s2[:, tq:], 0.0) * wt[2 * p + 1:2 * p + 2, :])
            acc = jnp.where(acc == 0.0, 0.0, acc)
            bits = pltpu.bitcast(acc, I32)
            key = bits ^ ((bits >> 31) & 0x7FFFFFFF)
            if causal:
                admissible = j * tk + rb * sub + key_iota[:sub] <= q_pos
                acc = jnp.where(admissible, acc, -jnp.inf)
                key = jnp.where(admissible, key, INT_MIN)
            sc_ref[j, rb * sub:(rb + 1) * sub, :] = acc
            half = sub // 2
            words = slice(rb * half, (rb + 1) * half)
            key_a, key_b = key[:half], key[half:]
            hi_ref[j, words, :] = lax.shift_right_logical(key_a, 16) | (key_b & -(1 << 16))
            lo_ref[j, words, :] = ((key_a & 0xFFFF) | jnp.left_shift(key_b, 16)) ^ FLIP_BIT15_BOTH

    def score_tile_pair(causal):
        def body(jj, carry):
            score_tile(2 * jj, causal)
            score_tile(2 * jj + 1, causal)
            return carry
        return body

    n_free = q0 // (2 * tk)
    lax.fori_loop(0, n_free, score_tile_pair(False), 0)
    lax.fori_loop(n_free, (n_act + 1) // 2, score_tile_pair(True), 0)

    def packed16(c):
        word = (c & 0xFFFF) | jnp.left_shift(c, 16)
        return pltpu.bitcast(jnp.broadcast_to(word, (8, tq)), jnp.int16)[0:1, :]

    def count16(w_ref, c):
        c16 = packed16(c)
        one = jnp.ones((), jnp.int16)
        zero = jnp.zeros((), jnp.int16)

        def body(jj, acc):
            for j in (2 * jj, 2 * jj + 1):
                hit = jnp.where(pltpu.bitcast(w_ref[j], jnp.int16) >= c16, one, zero)
                for r in range(tk // COUNT_ROWS16):
                    acc = acc + hit[r * COUNT_ROWS16:(r + 1) * COUNT_ROWS16]
            return acc
        acc = lax.fori_loop(0, (n_act + 1) // 2, body, jnp.zeros((COUNT_ROWS16, tq), jnp.int16))
        return jnp.sum(acc.astype(I32).astype(F32), axis=0, keepdims=True)

    def count(pred):
        def body(j, acc):
            hit = jnp.where(pred(sc_ref[j], j), 1.0, 0.0)
            return acc + hit.reshape(tk // COUNT_ROWS, COUNT_ROWS, tq).sum(axis=0)
        acc = lax.fori_loop(0, n_act, body, jnp.zeros((COUNT_ROWS, tq), F32))
        return jnp.sum(acc, axis=0, keepdims=True)

    kf = float(top_k)
    n_adm = (q_pos + 1).astype(F32)
    def try_value(w_ref, cand, val, n_keep, n_above):
        n = n_above + count16(w_ref, cand)
        ok = n >= kf
        return jnp.where(ok, cand, val), jnp.where(ok, n, n_keep)

    low16 = jnp.full((1, tq), -HALF16, I32)
    t_hi, n_ge = try_value(hi_ref, jnp.zeros((1, tq), I32), low16, n_adm, 0.0)
    for bit in range(14, -1, -1):
        t_hi, n_ge = try_value(hi_ref, t_hi | (1 << bit), t_hi, n_ge, 0.0)
    n_above = jnp.where(t_hi < HALF16 - 1, count16(hi_ref, jnp.minimum(t_hi + 1, HALF16 - 1)), 0.0)
    t_hi16 = packed16(t_hi)

    def keep_equal_upper(j, carry):
        same = pltpu.bitcast(hi_ref[j], jnp.int16) == t_hi16
        lo = jnp.where(same, pltpu.bitcast(lo_ref[j], jnp.int16), jnp.full((), -HALF16, jnp.int16))
        lo_ref[j] = pltpu.bitcast(lo, I32)
        return carry

    lax.fori_loop(0, n_act, keep_equal_upper, 0)
    group = 4

    def more(c):
        return jnp.logical_and(c[0] >= 0, jnp.max(c[2]) > kf)

    def bit_group(c):
        top, val, n_keep = c
        for d in range(group):
            bit = jnp.maximum(top - d, 0)
            cand = jnp.where(top - d == 15, 0, val | jnp.left_shift(jnp.int32(1), bit))
            val, n_keep = try_value(lo_ref, cand, val, n_keep, n_above)
        return top - group, val, n_keep

    _, t_lo, n_ge = lax.while_loop(more, bit_group, (jnp.int32(15), low16, n_ge))
    thr_key = jnp.left_shift(t_hi, 16) | (t_lo + HALF16)
    few = n_adm <= kf
    thr = jnp.where(few, float(jnp.finfo(F32).min), _key_to_float(thr_key))
    n_ge = jnp.where(few, n_adm, n_ge)
    ambiguous = jnp.max(n_ge) > kf
    n_bits = max(1, (seq - 1).bit_length())

    def tie_cut():
        need = kf - count(lambda s, j: s > thr)

        def step(b, x):
            cand = x | jnp.left_shift(jnp.int32(1), n_bits - 1 - b)
            cnt = count(lambda s, j: jnp.where(s == thr, j * tk + key_iota, seq) < cand)
            return jnp.where(cnt < need, cand, x)
        return lax.fori_loop(0, n_bits, step, jnp.zeros((1, tq), I32))

    def write_with_ties():
        cut = tie_cut()

        def write_mask(j, carry):
            s = sc_ref[j]
            tie_pos = jnp.where(s == thr, j * tk + key_iota, seq + 1)
            sel = jnp.where(s > thr, 1, jnp.where(tie_pos <= cut, 1, 0))
            mask_ref[0, j] = sel.astype(mask_ref.dtype)
            return carry
        return lax.fori_loop(0, n_act, write_mask, 0)

    def write_plain():
        def write_mask(j, carry):
            mask_ref[0, j] = jnp.where(sc_ref[j] >= thr, 1, 0).astype(mask_ref.dtype)
            return carry
        return lax.fori_loop(0, n_act, write_mask, 0)

    def write_zero(j, carry):
        mask_ref[0, j] = jnp.zeros((tk, tq), mask_ref.dtype)
        return carry

    lax.cond(ambiguous, write_with_ties, write_plain)
    lax.fori_loop(n_act, n_kt, write_zero, 0)


def _topk_mask(qi, wt, ki, top_k, tk):
    b, s, iw = qi.shape
    n_heads = iw // IDX_DIM
    tq = V7X_LANES
    n_kt = s // tk
    assert n_kt % 2 == 0, "score tiles are computed in pairs"
    return pl.pallas_call(
        functools.partial(_topk_mask_body, tq=tq, tk=tk, n_heads=n_heads, top_k=top_k, seq=s),
        grid=(b, s // tq),
        in_specs=[pl.BlockSpec((1, tq, iw), lambda bi, i: (bi, i, 0)),
                  pl.BlockSpec((1, n_heads, tq), lambda bi, i: (bi, 0, i)),
                  pl.BlockSpec((1, n_kt, tk, IDX_DIM), lambda bi, i: (bi, 0, 0, 0))],
        out_specs=pl.BlockSpec((1, n_kt, tk, tq), lambda bi, i: (bi, 0, 0, i)),
        out_shape=jax.ShapeDtypeStruct((b, n_kt, tk, s), jnp.int8),
        scratch_shapes=[pltpu.VMEM((n_kt, tk, tq), F32), pltpu.VMEM((n_kt, tk // 2, tq), I32),
                        pltpu.VMEM((n_kt, tk // 2, tq), I32)],
        compiler_params=_cparams("topk_mask", ("parallel", "arbitrary")),
        name="topk_mask",
    )(qi, wt, ki)


def _attn_body(i_tab, j_tab, q_ref, k_ref, vt_ref, mask_ref, o_ref, m_ref, l_ref, acc_ref, bias_ref, s_ref, *,
               tq, tk, n_heads):
    t = pl.program_id(1)
    i = i_tab[t]
    j = j_tab[t]
    hd = ATT_HEAD_DIM
    blk = V7X_LANES

    @pl.when(j == 0)
    def _():
        m_ref[...] = jnp.full(m_ref.shape, NEG_BIG, F32)
        l_ref[...] = jnp.zeros_like(l_ref)
        acc_ref[...] = jnp.zeros_like(acc_ref)

    bias_ref[...] = jnp.where(mask_ref[0, 0].astype(F32) != 0.0, 0.0, NEG_BIG)
    lane = lax.broadcasted_iota(I32, (1, 2 * hd), 1)
    half_sel = [jnp.where(lane < hd, 1.0, 0.0).astype(BF16), jnp.where(lane >= hd, 1.0, 0.0).astype(BF16)]

    key_blocks = [slice(kb * blk, (kb + 1) * blk) for kb in range(tk // blk)]
    ones_rows = jnp.ones((BF16_SUBLANES, blk), BF16)
    m_tile = {}
    base = jnp.minimum(j, 0)

    def scores(h):
        pair = slice((h // 2) * 2 * hd, (h // 2 + 1) * 2 * hd)
        qh = q_ref[0, :, pair] * half_sel[h % 2]
        top = None
        for ks in key_blocks:
            s = _dot_nt(k_ref[0, ks, pair], qh) + bias_ref[ks, :]
            s_ref[base + h, ks, :] = s
            bm = jnp.max(s, axis=0, keepdims=True)
            top = bm if top is None else jnp.maximum(top, bm)
        m_tile[h] = top

    def update(h):
        hs = slice(h * hd, (h + 1) * hd)
        m = m_ref[h:h + 1, :]
        m_new = jnp.maximum(m, m_tile[h])
        alpha = jnp.exp(m - m_new)
        pv = None
        for ks in key_blocks:
            p = jnp.exp(s_ref[base + h, ks, :] - m_new).astype(BF16)
            lhs = jnp.concatenate([vt_ref[0, hs, ks], ones_rows], axis=0)
            d = jnp.dot(lhs, p, preferred_element_type=F32)
            pv = d if pv is None else pv + d
        l_ref[h:h + 1, :] = alpha * l_ref[h:h + 1, :] + pv[hd:hd + 1, :]
        m_ref[h:h + 1, :] = m_new
        acc_ref[hs, :] = alpha * acc_ref[hs, :] + pv[:hd, :]

    for h in range(n_heads):
        scores(h)
    for h in range(n_heads):
        update(h)

    @pl.when(j == ((i + 1) * tq - 1) // tk)
    def _():
        for h in range(n_heads):
            hs = slice(h * hd, (h + 1) * hd)
            o_ref[0, hs, :] = (acc_ref[hs, :] / l_ref[h:h + 1, :]).astype(o_ref.dtype)


def _masked_attention(q, k, vt, mask, tk):
    b, s, aw = q.shape
    n_heads = aw // ATT_HEAD_DIM
    tq = _pick_tile(s, 512, V7X_LANES)
    pairs = [(i, j) for i in range(s // tq) for j in range(((i + 1) * tq - 1) // tk + 1)]
    i_tab = jnp.asarray([p[0] for p in pairs], I32)
    j_tab = jnp.asarray([p[1] for p in pairs], I32)
    grid_spec = pltpu.PrefetchScalarGridSpec(
        num_scalar_prefetch=2,
        grid=(b, len(pairs)),
        in_specs=[pl.BlockSpec((1, tq, aw), lambda bi, t, it, jt: (bi, it[t], 0)),
                  pl.BlockSpec((1, tk, aw), lambda bi, t, it, jt: (bi, jt[t], 0)),
                  pl.BlockSpec((1, aw, tk), lambda bi, t, it, jt: (bi, 0, jt[t])),
                  pl.BlockSpec((1, 1, tk, tq), lambda bi, t, it, jt: (bi, jt[t], 0, it[t]))],
        out_specs=pl.BlockSpec((1, aw, tq), lambda bi, t, it, jt: (bi, 0, it[t])),
        scratch_shapes=[pltpu.VMEM((n_heads, tq), F32), pltpu.VMEM((n_heads, tq), F32),
                        pltpu.VMEM((aw, tq), F32), pltpu.VMEM((tk, tq), F32),
                        pltpu.VMEM((n_heads, tk, tq), F32)],
    )
    return pl.pallas_call(
        functools.partial(_attn_body, tq=tq, tk=tk, n_heads=n_heads),
        grid_spec=grid_spec,
        out_shape=jax.ShapeDtypeStruct((b, aw, s), BF16),
        compiler_params=_cparams("masked_attn", ("parallel", "arbitrary")),
        name="masked_attn",
    )(i_tab, j_tab, q, k, vt, mask)


def _outproj_body(x_ref, *refs):
    n_in = (len(refs) - 1) // 2
    acc = x_ref[...]
    for y_ref, w_ref in zip(refs[:n_in], refs[n_in:2 * n_in]):
        acc = acc + jnp.dot(y_ref[...], w_ref[...], preferred_element_type=F32)
    refs[-1][...] = acc


def _outproj(x, ys, ws):
    n, d = x.shape
    tm = _pick_tile(n, 512, 8)
    row = lambda i: (i, 0)
    fixed = lambda i: (0, 0)
    return pl.pallas_call(
        _outproj_body,
        grid=(n // tm,),
        in_specs=[pl.BlockSpec((tm, d), row)] + [pl.BlockSpec((tm, y.shape[1]), row) for y in ys]
                 + [pl.BlockSpec(w.shape, fixed) for w in ws],
        out_specs=pl.BlockSpec((tm, d), row),
        out_shape=jax.ShapeDtypeStruct((n, d), F32),
        compiler_params=_cparams("outproj", ("parallel",)),
        name="outproj",
    )(x, *ys, *ws)


def _od_prep_body(x_ref, g_ref, w_ref, cw_ref, alog_ref, dtb_ref,
                  q_ref, k_ref, v_ref, z_ref, bg_ref, p_ref, *, tiles_per_seq, n_heads):
    tm = x_ref.shape[0]
    hist = DN_KERNEL_ROWS
    d = n_heads * DN_HEAD_DIM
    _project_rows(x_ref, g_ref, w_ref, p_ref, tiles_per_seq)
    ktaps = cw_ref.shape[0]
    for part, out_ref in enumerate((q_ref, k_ref, v_ref)):
        for h in range(n_heads):
            col = part * d + h * DN_HEAD_DIM
            block = p_ref[0:hist + tm, col:col + DN_HEAD_DIM]
            conv = None
            for j in range(ktaps):
                shift = ktaps - 1 - j
                delayed = block if shift == 0 else pltpu.roll(block, shift, 0)
                term = delayed[hist:hist + tm] * cw_ref[j:j + 1, col:col + DN_HEAD_DIM]
                conv = term if conv is None else conv + term
            u = _silu(conv)
            if part < 2:
                u = u * lax.rsqrt(jnp.sum(u * u, axis=-1, keepdims=True) + NORM_EPS)
            if part == 0:
                u = u * DN_HEAD_DIM ** -0.5
            out_ref[:, h * DN_HEAD_DIM:(h + 1) * DN_HEAD_DIM] = u
    z_ref[...] = p_ref[hist:hist + tm, 3 * d:4 * d]
    slab = p_ref[hist:hist + tm, 4 * d:4 * d + V7X_LANES]
    lane = lax.broadcasted_iota(I32, slab.shape, 1)
    xs = slab + dtb_ref[...]
    softplus = jnp.maximum(xs, 0.0) + jnp.log1p(jnp.exp(-jnp.abs(xs)))
    bg_ref[...] = jnp.where(lane < n_heads, jax.nn.sigmoid(slab), -jnp.exp(alog_ref[...]) * softplus)


def _od_prep(x, gain, w_in, conv_w, a_log, dt_bias, seq):
    n, d = x.shape
    n_heads = d // DN_HEAD_DIM
    ppad = w_in.shape[1]
    tm = _pick_tile(seq, 256, 8)
    row = lambda i: (i, 0)
    fixed = lambda i: (0, 0)
    pad = lambda t: jnp.pad(t.astype(F32), (n_heads, V7X_LANES - 2 * n_heads)).reshape(1, V7X_LANES)
    outs = [jax.ShapeDtypeStruct((n, d), F32)] * 4 + [jax.ShapeDtypeStruct((n, V7X_LANES), F32)]
    return pl.pallas_call(
        functools.partial(_od_prep_body, tiles_per_seq=seq // tm, n_heads=n_heads),
        grid=(n // tm,),
        in_specs=[pl.BlockSpec((tm, d), row), pl.BlockSpec((1, d), fixed), pl.BlockSpec((d, ppad), fixed),
                  pl.BlockSpec(conv_w.shape, fixed), pl.BlockSpec((1, V7X_LANES), fixed),
                  pl.BlockSpec((1, V7X_LANES), fixed)],
        out_specs=[pl.BlockSpec((tm, d), row)] * 4 + [pl.BlockSpec((tm, V7X_LANES), row)],
        out_shape=outs,
        scratch_shapes=[pltpu.VMEM((tm + DN_KERNEL_ROWS, ppad), F32)],
        compiler_params=_cparams("od_prep", ("arbitrary",)),
        name="od_prep",
    )(x, gain.reshape(1, d), w_in, conv_w, pad(a_log), pad(dt_bias))


def _delta_body(q_ref, k_ref, v_ref, z_ref, bg_ref, bgt_ref, gain_ref, y_ref, state_ref, *, n_heads, n_chunks):
    c_len = DN_CHUNK
    hd = DN_HEAD_DIM
    heads = range(n_heads)

    @pl.when(pl.program_id(1) == 0)
    def _():
        state_ref[...] = jnp.zeros_like(state_ref)

    ri = lax.broadcasted_iota(I32, (c_len, c_len), 0)
    ci = lax.broadcasted_iota(I32, (c_len, c_len), 1)
    lower = ci <= ri
    strict = ci < ri
    eye = jnp.where(ci == ri, 1.0, 0.0)

    group = max(g for g in (1, 2, 4) if n_chunks % g == 0)

    def group_step(cg, carry):
        chunk_rows = [pl.ds(pl.multiple_of((cg * group + g) * c_len, c_len), c_len) for g in range(group)]
        bg = [bg_ref[0, chunk_rows[g], :] for g in range(group)]
        bgt = [bgt_ref[0, cg * group + g] for g in range(group)]
        units = [(g, h) for g in range(group) for h in heads]
        cols = lambda h: slice(h * hd, (h + 1) * hd)
        q = [q_ref[0, chunk_rows[g], cols(h)] for g, h in units]
        k = [k_ref[0, chunk_rows[g], cols(h)] for g, h in units]
        v = [v_ref[0, chunk_rows[g], cols(h)] for g, h in units]
        n_units = range(len(units))
        beta = [bg[g][:, h:h + 1] for g, h in units]
        gc_col = [jnp.sum(jnp.where(lower, bgt[g][n_heads + h:n_heads + h + 1, :], 0.0), axis=1, keepdims=True)
                  for g, h in units]
        gc_row = [jnp.sum(jnp.where(ri <= ci, bg[g][:, n_heads + h:n_heads + h + 1], 0.0), axis=0, keepdims=True)
                  for g, h in units]
        decay = [jnp.where(lower, jnp.exp(jnp.where(lower, gc_col[u] - gc_row[u], 0.0)), 0.0) for u in n_units]
        kb = [k[u] * beta[u] for u in n_units]
        kbf = [k[u].astype(BF16) for u in n_units]
        kq = [_dot_nt(jnp.concatenate([kb[u], q[u]], axis=0).astype(BF16), kbf[u]) for u in n_units]
        intra = [(kq[u][c_len:] * decay[u]).astype(BF16) for u in n_units]
        pw = [jnp.where(strict, -(kq[u][:c_len] * decay[u]), 0.0) for u in n_units]
        t_inv = [eye + pw[u] for u in n_units]
        span = 1
        while 2 * span < c_len:
            pwb = [p.astype(BF16) for p in pw]
            pw = [jnp.dot(pwb[u], pwb[u], preferred_element_type=F32) for u in n_units]
            t_inv = [t_inv[u] + jnp.dot(t_inv[u].astype(BF16), pw[u].astype(BF16), preferred_element_type=F32)
                     for u in n_units]
            span *= 2
        e_gc = [jnp.exp(gc_col[u]) for u in n_units]
        uw = [jnp.dot(t_inv[u].astype(BF16),
                      jnp.concatenate([v[u] * beta[u], kb[u] * e_gc[u]], axis=1).astype(BF16),
                      preferred_element_type=F32) for u in n_units]
        g_last = [gc_col[u][c_len - 1:c_len, :] for u in n_units]
        k_dec_t = [(k[u] * jnp.exp(g_last[u] - gc_col[u])).T.astype(BF16) for u in n_units]
        wq = [jnp.concatenate([uw[u][:, hd:], q[u] * e_gc[u]], axis=0).astype(BF16) for u in n_units]
        state = [state_ref[h] for h in heads]
        for g in range(group):
            us = [g * n_heads + h for h in heads]
            ws = [jnp.dot(wq[u], state[h].astype(BF16), preferred_element_type=F32) for h, u in zip(heads, us)]
            v_new = [(uw[u][:, :hd] - ws[h][:c_len]).astype(BF16) for h, u in zip(heads, us)]
            out = [ws[h][c_len:] + jnp.dot(intra[u], v_new[h], preferred_element_type=F32)
                   for h, u in zip(heads, us)]
            state = [state[h] * jnp.exp(g_last[u]) + jnp.dot(k_dec_t[u], v_new[h], preferred_element_type=F32)
                     for h, u in zip(heads, us)]
            for h in heads:
                o = out[h] * lax.rsqrt(jnp.mean(out[h] * out[h], axis=-1, keepdims=True) + NORM_EPS) * gain_ref[...]
                zh = z_ref[0, chunk_rows[g], cols(h)]
                y_ref[0, chunk_rows[g], cols(h)] = (o * _silu(zh)).astype(y_ref.dtype)
        for h in heads:
            state_ref[h] = state[h]
        return carry

    lax.fori_loop(0, n_chunks // group, group_step, 0)


def _delta_rule(q, k, v, z, bg, bgt, o_gain):
    b, s, d = q.shape
    n_heads = d // DN_HEAD_DIM
    rows = _pick_tile(s, 512, DN_CHUNK)
    n_chunks = rows // DN_CHUNK
    blk = lambda bi, i: (bi, i, 0)
    return pl.pallas_call(
        functools.partial(_delta_body, n_heads=n_heads, n_chunks=n_chunks),
        grid=(b, s // rows),
        in_specs=[pl.BlockSpec((1, rows, d), blk)] * 4 + [
            pl.BlockSpec((1, rows, V7X_LANES), blk),
            pl.BlockSpec((1, n_chunks, 2 * n_heads, DN_CHUNK), lambda bi, i: (bi, i, 0, 0)),
            pl.BlockSpec((1, DN_HEAD_DIM), lambda bi, i: (0, 0))],
        out_specs=pl.BlockSpec((1, rows, d), blk),
        out_shape=jax.ShapeDtypeStruct((b, s, d), BF16),
        scratch_shapes=[pltpu.VMEM((n_heads, DN_HEAD_DIM, DN_HEAD_DIM), F32)],
        compiler_params=_cparams("delta_rule", ("parallel", "arbitrary")),
        name="delta_rule",
    )(q, k, v, z, bg, bgt, o_gain.reshape(1, DN_HEAD_DIM))


def _rope_tables(positions):
    half = ROPE_DIM // 2
    inv_freq = ROPE_THETA ** (-jnp.arange(0, ROPE_DIM, 2, dtype=F32) / ROPE_DIM)
    ang = positions.astype(F32).reshape(-1, 1) * inv_freq
    cos, sin = jnp.cos(ang), jnp.sin(ang)
    n = ang.shape[0]
    rest = ATT_HEAD_DIM - ROPE_DIM
    zeros = lambda m: jnp.zeros((n, m), F32)
    c = jnp.concatenate([cos, cos, jnp.ones((n, rest), F32)], axis=1)
    sa = jnp.concatenate([-sin, zeros(half + rest)], axis=1)
    sb = jnp.concatenate([zeros(half), sin, zeros(rest)], axis=1)
    reps = V7X_LANES // ATT_HEAD_DIM
    return tuple(jnp.tile(t, (1, reps)) for t in (c, sa, sb))


def _pad_cols(w, mult):
    pad = (-w.shape[-1]) % mult
    return jnp.pad(w, [(0, 0)] * (w.ndim - 1) + [(0, pad)])


def kernel(x, positions, norm_gain, ffn_w_gate, ffn_w_up, ffn_w_down, ev_w_in, ev_conv_w, ev_w_out,
           od_w_in, od_conv_w, od_a_log, od_dt_bias, od_o_gain, od_w_out, final_gain):
    bsz, seq, d = x.shape
    depth = norm_gain.shape[0]
    n = bsz * seq
    assert d % (2 * V7X_LANES) == 0 and seq % V7X_LANES == 0
    top_k = min(TOPK_MAX, seq // 4)
    tk = _pick_tile(seq, 512, V7X_LANES)
    dff = ffn_w_gate.shape[-1]

    wg = ffn_w_gate.reshape(depth * 2, d, dff)
    wu = ffn_w_up.reshape(depth * 2, d, dff)
    wd = ffn_w_down.reshape(depth * 2, dff, d)
    ev_in = _pad_cols(ev_w_in, V7X_LANES).astype(BF16)
    ev_out = ev_w_out.astype(BF16)
    od_in = _pad_cols(od_w_in, V7X_LANES).astype(BF16)
    od_out = od_w_out.astype(BF16)
    tables = _rope_tables(positions)
    half_w = d // 2

    x = x.reshape(n, d)
    for layer in range(depth):
        i = layer // 2
        x = _ffn(x, norm_gain[layer, 0], wg, wu, wd, 2 * layer)
        if layer % 2 == 0:
            yc, q, k, v, qi, kiw = _ev_prep(x, norm_gain[layer, 1], ev_in[i], ev_conv_w[i], tables, seq)
            kiw3 = kiw.reshape(bsz, seq, V7X_LANES)
            ki = kiw3[:, :, :IDX_DIM].astype(BF16).reshape(bsz, seq // tk, tk, IDX_DIM)
            n_idx = half_w // IDX_DIM
            wt = jnp.swapaxes(kiw3[:, :, IDX_DIM:IDX_DIM + n_idx], 1, 2)
            to3 = lambda t: t.reshape(bsz, seq, half_w)
            mask = _topk_mask(to3(qi), wt, ki, top_k, tk)
            ya_t = _masked_attention(to3(q), to3(k), jnp.swapaxes(to3(v), 1, 2), mask, tk)
            ya = jnp.swapaxes(ya_t, 1, 2).reshape(n, half_w)
            x = _outproj(x, [yc, ya], [ev_out[i, :half_w], ev_out[i, half_w:]])
        else:
            q, k, v, z, bg = _od_prep(x, norm_gain[layer, 1], od_in[i], od_conv_w[i], od_a_log[i],
                                      od_dt_bias[i], seq)
            n_heads = d // DN_HEAD_DIM
            bg3 = bg.reshape(bsz, seq, V7X_LANES)
            bgt = jnp.swapaxes(bg3[:, :, :2 * n_heads].reshape(bsz, seq // DN_CHUNK, DN_CHUNK, 2 * n_heads),
                               2, 3)
            to3 = lambda t: t.reshape(bsz, seq, d)
            y = _delta_rule(to3(q), to3(k), to3(v), to3(z), bg3, bgt, od_o_gain[i]).reshape(n, d)
            x = _outproj(x, [y], [od_out[i]])
        x = _ffn(x, norm_gain[layer, 2], wg, wu, wd, 2 * layer + 1,
                 final_gain=final_gain if layer == depth - 1 else None)
    return x.reshape(bsz, seq, d)
```

```python
import functools
import math

import jax
import jax.numpy as jnp
from jax import lax
from jax.experimental import pallas as pl
from jax.experimental.pallas import tpu as pltpu

F32 = jnp.float32
BF16 = jnp.bfloat16
I32 = jnp.int32

NORM_EPS = 1e-6
ATT_HEAD_DIM = 64
ATT_SCALE = ATT_HEAD_DIM ** -0.5
assert ATT_SCALE == 2.0 ** round(math.log2(ATT_SCALE)), "folding the scale into bf16 q needs a power of two"
IDX_DIM = 64
DN_HEAD_DIM = 128
DN_CHUNK = 64
DN_KERNEL_ROWS = 8
TOPK_MAX = 256
ROPE_DIM = 16
ROPE_THETA = 500000.0

V7X_LANES = 128
BF16_SUBLANES = 16
V7X_VMEM_BYTES = 64 * 2**20
INT_MIN = -(2**31)
NEG_BIG = -1e30


VMEM_MIB = {"ffn": 48, "ev_prep": 48, "topk_mask": 48, "masked_attn": 32, "outproj": 32, "od_prep": 56,
            "delta_rule": 40}


def _cparams(kernel_name, semantics):
    vmem_bytes = VMEM_MIB[kernel_name] * 2**20
    assert vmem_bytes < V7X_VMEM_BYTES
    return pltpu.CompilerParams(dimension_semantics=semantics, vmem_limit_bytes=vmem_bytes)


def _pick_tile(n, target, mult):
    best = None
    t = mult
    while t <= min(n, target):
        if n % t == 0:
            best = t
        t += mult
    assert best is not None, (n, target, mult)
    return best


def _rmsnorm(x, gain):
    ms = jnp.mean(x * x, axis=-1, keepdims=True)
    return x * lax.rsqrt(ms + NORM_EPS) * gain


def _silu(x):
    return x * jax.nn.sigmoid(x)


def _dot_nt(a, b):
    return lax.dot_general(a, b, (((1,), (1,)), ((), ())), preferred_element_type=F32)


def _ffn_body(x_ref, g_ref, wg_ref, wu_ref, wd_ref, *rest, final):
    if final:
        fg_ref, o_ref, h_ref, acc_ref = rest
    else:
        o_ref, h_ref, acc_ref = rest
    f = pl.program_id(1)

    @pl.when(f == 0)
    def _():
        h_ref[...] = _rmsnorm(x_ref[...], g_ref[...]).astype(BF16)
        acc_ref[...] = jnp.zeros_like(acc_ref)

    h = h_ref[...]
    a = jnp.dot(h, wg_ref[...].astype(BF16), preferred_element_type=F32)
    b = jnp.dot(h, wu_ref[...].astype(BF16), preferred_element_type=F32)
    t = (_silu(a) * b).astype(BF16)
    acc_ref[...] += jnp.dot(t, wd_ref[...].astype(BF16), preferred_element_type=F32)

    @pl.when(f == pl.num_programs(1) - 1)
    def _():
        y = x_ref[...] + 0.5 * acc_ref[...]
        if final:
            y = _rmsnorm(y, fg_ref[...])
        o_ref[...] = y


def _ffn(x, gain, wg, wu, wd, widx, final_gain=None):
    n, d = x.shape
    dff = wg.shape[-1]
    tm = _pick_tile(n, 1024, 8)
    tf = _pick_tile(dff, 512, V7X_LANES)
    final = final_gain is not None
    in_specs = [
        pl.BlockSpec((tm, d), lambda i, f: (i, 0)),
        pl.BlockSpec((1, d), lambda i, f: (0, 0)),
        pl.BlockSpec((None, d, tf), lambda i, f: (widx, 0, f)),
        pl.BlockSpec((None, d, tf), lambda i, f: (widx, 0, f)),
        pl.BlockSpec((None, tf, d), lambda i, f: (widx, f, 0)),
    ]
    args = [x, gain.reshape(1, d), wg, wu, wd]
    if final:
        in_specs.append(pl.BlockSpec((1, d), lambda i, f: (0, 0)))
        args.append(final_gain.reshape(1, d))
    return pl.pallas_call(
        functools.partial(_ffn_body, final=final),
        grid=(n // tm, dff // tf),
        in_specs=in_specs,
        out_specs=pl.BlockSpec((tm, d), lambda i, f: (i, 0)),
        out_shape=jax.ShapeDtypeStruct((n, d), F32),
        scratch_shapes=[pltpu.VMEM((tm, d), BF16), pltpu.VMEM((tm, d), F32)],
        compiler_params=_cparams("ffn", ("parallel", "arbitrary")),
        name="ffn_final" if final else "ffn",
    )(*args)


def _project_rows(x_ref, g_ref, w_ref, p_ref, tiles_per_seq):
    i = pl.program_id(0)
    tm = x_ref.shape[0]
    hist = DN_KERNEL_ROWS

    @pl.when(i % tiles_per_seq == 0)
    def _():
        p_ref[0:hist, :] = jnp.zeros((hist, p_ref.shape[1]), F32)

    @pl.when(i % tiles_per_seq != 0)
    def _():
        p_ref[0:hist, :] = p_ref[tm:tm + hist, :]

    h = _rmsnorm(x_ref[...], g_ref[...]).astype(BF16)
    p_ref[hist:hist + tm, :] = jnp.dot(h, w_ref[...], preferred_element_type=F32)


def _rope(x, c, sa, sb):
    n = x.shape[1]
    reps = n // V7X_LANES
    if reps > 1:
        c, sa, sb = (jnp.concatenate([t] * reps, axis=1) for t in (c, sa, sb))
    half = ROPE_DIM // 2
    x_up = pltpu.roll(x, n - half, 1)
    x_dn = pltpu.roll(x, half, 1)
    return x * c + x_up * sa + x_dn * sb


def _ev_prep_body(x_ref, g_ref, w_ref, cw_ref, c_ref, sa_ref, sb_ref,
                  yc_ref, q_ref, k_ref, v_ref, qi_ref, kiw_ref, p_ref, *, tiles_per_seq, width, w_scale):
    tm = x_ref.shape[0]
    hist = DN_KERNEL_ROWS
    _project_rows(x_ref, g_ref, w_ref, p_ref, tiles_per_seq)
    wd = width

    def rows(shift, col):
        return p_ref[hist - shift:hist - shift + tm, col * wd:(col + 1) * wd]

    ktaps = cw_ref.shape[0]
    u = p_ref[0:hist + tm, wd:2 * wd] * p_ref[0:hist + tm, 2 * wd:3 * wd]
    conv = None
    for j in range(ktaps):
        shift = ktaps - 1 - j
        delayed = u if shift == 0 else pltpu.roll(u, shift, 0)
        term = delayed[hist:hist + tm] * cw_ref[j:j + 1, :]
        conv = term if conv is None else conv + term
    yc_ref[...] = (rows(0, 0) * conv).astype(BF16)

    c, sa, sb = c_ref[...], sa_ref[...], sb_ref[...]
    q_ref[...] = (_rope(rows(0, 3), c, sa, sb) * ATT_SCALE).astype(BF16)
    k_ref[...] = _rope(rows(0, 4), c, sa, sb).astype(BF16)
    v_ref[...] = rows(0, 5).astype(BF16)
    qi_ref[...] = _rope(rows(0, 6), c, sa, sb).astype(BF16)
    slab = p_ref[hist:hist + tm, 7 * wd:7 * wd + V7X_LANES]
    lane = lax.broadcasted_iota(I32, slab.shape, 1)
    kiw_ref[...] = jnp.where(lane < IDX_DIM, _rope(slab, c, sa, sb), slab * w_scale)


def _ev_prep(x, gain, w_in, conv_w, tables, seq):
    n, d = x.shape
    wd = d // 2
    ppad = w_in.shape[1]
    tm = _pick_tile(seq, 512, 8)
    row = lambda i: (i, 0)
    fixed = lambda i: (0, 0)
    n_idx_heads = wd // IDX_DIM
    outs = [jax.ShapeDtypeStruct((n, wd), BF16)] * 5 + [jax.ShapeDtypeStruct((n, V7X_LANES), F32)]
    return pl.pallas_call(
        functools.partial(_ev_prep_body, tiles_per_seq=seq // tm, width=wd,
                          w_scale=float(n_idx_heads * IDX_DIM) ** -0.5),
        grid=(n // tm,),
        in_specs=[pl.BlockSpec((tm, d), row), pl.BlockSpec((1, d), fixed), pl.BlockSpec((d, ppad), fixed, pipeline_mode=pl.Buffered(1)),
                  pl.BlockSpec(conv_w.shape, fixed)] + [pl.BlockSpec((tm, V7X_LANES), row)] * 3,
        out_specs=[pl.BlockSpec((tm, wd), row)] * 5 + [pl.BlockSpec((tm, V7X_LANES), row)],
        out_shape=outs,
        scratch_shapes=[pltpu.VMEM((tm + DN_KERNEL_ROWS, ppad), F32)],
        compiler_params=_cparams("ev_prep", ("arbitrary",)),
        name="ev_prep",
    )(x, gain.reshape(1, d), w_in, conv_w, *tables)


COUNT_ROWS = 64
COUNT_ROWS16 = 64
HALF16 = 1 << 15
FLIP_BIT15_BOTH = (HALF16 | HALF16 << 16) - (1 << 32)


def _key_to_float(key):
    return pltpu.bitcast(key ^ ((key >> 31) & 0x7FFFFFFF), F32)


def _topk_mask_body(qi_ref, wt_ref, ki_ref, mask_ref, sc_ref, hi_ref, lo_ref, *, tq, tk, n_heads, top_k, seq):
    i = pl.program_id(1)
    n_kt = seq // tk
    q0 = i * tq
    n_act = (q0 + tq + tk - 1) // tk
    q_pos = q0 + lax.broadcasted_iota(I32, (1, tq), 1)
    key_iota = lax.broadcasted_iota(I32, (tk, tq), 0)
    wt = wt_ref[0]

    q_pairs = [jnp.concatenate([qi_ref[0, :, (2 * p) * IDX_DIM:(2 * p + 1) * IDX_DIM],
                                qi_ref[0, :, (2 * p + 1) * IDX_DIM:(2 * p + 2) * IDX_DIM]], axis=0)
               for p in range(n_heads // 2)]
    sub = V7X_LANES

    def score_tile(j, causal):
        for rb in range(tk // sub):
            kr = ki_ref[0, j, rb * sub:(rb + 1) * sub, :]
            pair_scores = [_dot_nt(kr, qp) for qp in q_pairs]
            acc = jnp.zeros((sub, tq), F32)
            for p, s2 in enumerate(pair_scores):
                acc = (acc + jnp.maximum(s2[:, :tq], 0.0) * wt[2 * p:2 * p + 1, :]
                       + jnp.maximum(s2[:, tq:], 0.0) * wt[2 * p + 1:2 * p + 2, :])
            acc = jnp.where(acc == 0.0, 0.0, acc)
            bits = pltpu.bitcast(acc, I32)
            key = bits ^ ((bits >> 31) & 0x7FFFFFFF)
            if causal:
                admissible = j * tk + rb * sub + key_iota[:sub] <= q_pos
                acc = jnp.where(admissible, acc, -jnp.inf)
                key = jnp.where(admissible, key, INT_MIN)
            sc_ref[j, rb * sub:(rb + 1) * sub, :] = acc
            half = sub // 2
            words = slice(rb * half, (rb + 1) * half)
            key_a, key_b = key[:half], key[half:]
            hi_ref[j, words, :] = lax.shift_right_logical(key_a, 16) | (key_b & -(1 << 16))
            lo_ref[j, words, :] = ((key_a & 0xFFFF) | jnp.left_shift(key_b, 16)) ^ FLIP_BIT15_BOTH

    def score_tile_pair(causal):
        def body(jj, carry):
            score_tile(2 * jj, causal)
            score_tile(2 * jj + 1, causal)
            return carry
        return body

    n_free = q0 // (2 * tk)
    lax.fori_loop(0, n_free, score_tile_pair(False), 0)
    lax.fori_loop(n_free, (n_act + 1) // 2, score_tile_pair(True), 0)

    def packed16(c):
        word = (c & 0xFFFF) | jnp.left_shift(c, 16)
        return pltpu.bitcast(jnp.broadcast_to(word, (8, tq)), jnp.int16)[0:1, :]

    def count16(w_ref, c):
        c16 = packed16(c)
        one = jnp.ones((), jnp.int16)
        zero = jnp.zeros((), jnp.int16)

        def body(jj, acc):
            for j in (2 * jj, 2 * jj + 1):
                hit = jnp.where(pltpu.bitcast(w_ref[j], jnp.int16) >= c16, one, zero)
                for r in range(tk // COUNT_ROWS16):
                    acc = acc + hit[r * COUNT_ROWS16:(r + 1) * COUNT_ROWS16]
            return acc
        acc = lax.fori_loop(0, (n_act + 1) // 2, body, jnp.zeros((COUNT_ROWS16, tq), jnp.int16))
        return jnp.sum(acc.astype(I32).astype(F32), axis=0, keepdims=True)

    def count(pred):
        def body(j, acc):
            hit = jnp.where(pred(sc_ref[j], j), 1.0, 0.0)
            return acc + hit.reshape(tk // COUNT_ROWS, COUNT_ROWS, tq).sum(axis=0)
        acc = lax.fori_loop(0, n_act, body, jnp.zeros((COUNT_ROWS, tq), F32))
        return jnp.sum(acc, axis=0, keepdims=True)

    kf = float(top_k)
    n_adm = (q_pos + 1).astype(F32)
    def try_value(w_ref, cand, val, n_keep, n_above):
        n = n_above + count16(w_ref, cand)
        ok = n >= kf
        return jnp.where(ok, cand, val), jnp.where(ok, n, n_keep)

    low16 = jnp.full((1, tq), -HALF16, I32)
    t_hi, n_ge = try_value(hi_ref, jnp.zeros((1, tq), I32), low16, n_adm, 0.0)
    for bit in range(14, -1, -1):
        t_hi, n_ge = try_value(hi_ref, t_hi | (1 << bit), t_hi, n_ge, 0.0)
    n_above = jnp.where(t_hi < HALF16 - 1, count16(hi_ref, jnp.minimum(t_hi + 1, HALF16 - 1)), 0.0)
    t_hi16 = packed16(t_hi)

    def keep_equal_upper(j, carry):
        same = pltpu.bitcast(hi_ref[j], jnp.int16) == t_hi16
        lo = jnp.where(same, pltpu.bitcast(lo_ref[j], jnp.int16), jnp.full((), -HALF16, jnp.int16))
        lo_ref[j] = pltpu.bitcast(lo, I32)
        return carry

    lax.fori_loop(0, n_act, keep_equal_upper, 0)
    group = 4

    def more(c):
        return jnp.logical_and(c[0] >= 0, jnp.max(c[2]) > kf)

    def bit_group(c):
        top, val, n_keep = c
        for d in range(group):
            bit = jnp.maximum(top - d, 0)
            cand = jnp.where(top - d == 15, 0, val | jnp.left_shift(jnp.int32(1), bit))
            val, n_keep = try_value(lo_ref, cand, val, n_keep, n_above)
        return top - group, val, n_keep

    _, t_lo, n_ge = lax.while_loop(more, bit_group, (jnp.int32(15), low16, n_ge))
    thr_key = jnp.left_shift(t_hi, 16) | (t_lo + HALF16)
    few = n_adm <= kf
    thr = jnp.where(few, float(jnp.finfo(F32).min), _key_to_float(thr_key))
    n_ge = jnp.where(few, n_adm, n_ge)
    ambiguous = jnp.max(n_ge) > kf
    n_bits = max(1, (seq - 1).bit_length())

    def tie_cut():
        need = kf - count(lambda s, j: s > thr)

        def step(b, x):
            cand = x | jnp.left_shift(jnp.int32(1), n_bits - 1 - b)
            cnt = count(lambda s, j: jnp.where(s == thr, j * tk + key_iota, seq) < cand)
            return jnp.where(cnt < need, cand, x)
        return lax.fori_loop(0, n_bits, step, jnp.zeros((1, tq), I32))

    def write_with_ties():
        cut = tie_cut()

        def write_mask(j, carry):
            s = sc_ref[j]
            tie_pos = jnp.where(s == thr, j * tk + key_iota, seq + 1)
            sel = jnp.where(s > thr, 1, jnp.where(tie_pos <= cut, 1, 0))
            mask_ref[0, j] = sel.astype(mask_ref.dtype)
            return carry
        return lax.fori_loop(0, n_act, write_mask, 0)

    def write_plain():
        def write_mask(j, carry):
            mask_ref[0, j] = jnp.where(sc_ref[j] >= thr, 1, 0).astype(mask_ref.dtype)
            return carry
        return lax.fori_loop(0, n_act, write_mask, 0)

    def write_zero(j, carry):
        mask_ref[0, j] = jnp.zeros((tk, tq), mask_ref.dtype)
        return carry

    lax.cond(ambiguous, write_with_ties, write_plain)
    lax.fori_loop(n_act, n_kt, write_zero, 0)


def _topk_mask(qi, wt, ki, top_k, tk):
    b, s, iw = qi.shape
    n_heads = iw // IDX_DIM
    tq = V7X_LANES
    n_kt = s // tk
    assert n_kt % 2 == 0, "score tiles are computed in pairs"
    return pl.pallas_call(
        functools.partial(_topk_mask_body, tq=tq, tk=tk, n_heads=n_heads, top_k=top_k, seq=s),
        grid=(b, s // tq),
        in_specs=[pl.BlockSpec((1, tq, iw), lambda bi, i: (bi, i, 0)),
                  pl.BlockSpec((1, n_heads, tq), lambda bi, i: (bi, 0, i)),
                  pl.BlockSpec((1, n_kt, tk, IDX_DIM), lambda bi, i: (bi, 0, 0, 0))],
        out_specs=pl.BlockSpec((1, n_kt, tk, tq), lambda bi, i: (bi, 0, 0, i)),
        out_shape=jax.ShapeDtypeStruct((b, n_kt, tk, s), jnp.int8),
        scratch_shapes=[pltpu.VMEM((n_kt, tk, tq), F32), pltpu.VMEM((n_kt, tk // 2, tq), I32),
                        pltpu.VMEM((n_kt, tk // 2, tq), I32)],
        compiler_params=_cparams("topk_mask", ("parallel", "arbitrary")),
        name="topk_mask",
    )(qi, wt, ki)


def _attn_body(i_tab, j_tab, q_ref, k_ref, vt_ref, mask_ref, o_ref, m_ref, l_ref, acc_ref, bias_ref, s_ref, *,
               tq, tk, n_heads):
    t = pl.program_id(1)
    i = i_tab[t]
    j = j_tab[t]
    hd = ATT_HEAD_DIM
    blk = V7X_LANES

    @pl.when(j == 0)
    def _():
        m_ref[...] = jnp.full(m_ref.shape, NEG_BIG, F32)
        l_ref[...] = jnp.zeros_like(l_ref)
        acc_ref[...] = jnp.zeros_like(acc_ref)

    bias_ref[...] = jnp.where(mask_ref[0, 0].astype(F32) != 0.0, 0.0, NEG_BIG)
    lane = lax.broadcasted_iota(I32, (1, 2 * hd), 1)
    half_sel = [jnp.where(lane < hd, 1.0, 0.0).astype(BF16), jnp.where(lane >= hd, 1.0, 0.0).astype(BF16)]

    key_blocks = [slice(kb * blk, (kb + 1) * blk) for kb in range(tk // blk)]
    ones_rows = jnp.ones((BF16_SUBLANES, blk), BF16)
    m_tile = {}
    base = jnp.minimum(j, 0)

    def scores(h):
        pair = slice((h // 2) * 2 * hd, (h // 2 + 1) * 2 * hd)
        qh = q_ref[0, :, pair] * half_sel[h % 2]
        top = None
        for ks in key_blocks:
            s = _dot_nt(k_ref[0, ks, pair], qh) + bias_ref[ks, :]
            s_ref[base + h, ks, :] = s
            bm = jnp.max(s, axis=0, keepdims=True)
            top = bm if top is None else jnp.maximum(top, bm)
        m_tile[h] = top

    def update(h):
        hs = slice(h * hd, (h + 1) * hd)
        m = m_ref[h:h + 1, :]
        m_new = jnp.maximum(m, m_tile[h])
        alpha = jnp.exp(m - m_new)
        pv = None
        for ks in key_blocks:
            p = jnp.exp(s_ref[base + h, ks, :] - m_new).astype(BF16)
            lhs = jnp.concatenate([vt_ref[0, hs, ks], ones_rows], axis=0)
            d = jnp.dot(lhs, p, preferred_element_type=F32)
            pv = d if pv is None else pv + d
        l_ref[h:h + 1, :] = alpha * l_ref[h:h + 1, :] + pv[hd:hd + 1, :]
        m_ref[h:h + 1, :] = m_new
        acc_ref[hs, :] = alpha * acc_ref[hs, :] + pv[:hd, :]

    for h in range(n_heads):
        scores(h)
    for h in range(n_heads):
        update(h)

    @pl.when(j == ((i + 1) * tq - 1) // tk)
    def _():
        for h in range(n_heads):
            hs = slice(h * hd, (h + 1) * hd)
            o_ref[0, hs, :] = (acc_ref[hs, :] / l_ref[h:h + 1, :]).astype(o_ref.dtype)


def _masked_attention(q, k, vt, mask, tk):
    b, s, aw = q.shape
    n_heads = aw // ATT_HEAD_DIM
    tq = _pick_tile(s, 512, V7X_LANES)
    pairs = [(i, j) for i in range(s // tq) for j in range(((i + 1) * tq - 1) // tk + 1)]
    i_tab = jnp.asarray([p[0] for p in pairs], I32)
    j_tab = jnp.asarray([p[1] for p in pairs], I32)
    grid_spec = pltpu.PrefetchScalarGridSpec(
        num_scalar_prefetch=2,
        grid=(b, len(pairs)),
        in_specs=[pl.BlockSpec((1, tq, aw), lambda bi, t, it, jt: (bi, it[t], 0)),
                  pl.BlockSpec((1, tk, aw), lambda bi, t, it, jt: (bi, jt[t], 0)),
                  pl.BlockSpec((1, aw, tk), lambda bi, t, it, jt: (bi, 0, jt[t])),
                  pl.BlockSpec((1, 1, tk, tq), lambda bi, t, it, jt: (bi, jt[t], 0, it[t]))],
        out_specs=pl.BlockSpec((1, aw, tq), lambda bi, t, it, jt: (bi, 0, it[t])),
        scratch_shapes=[pltpu.VMEM((n_heads, tq), F32), pltpu.VMEM((n_heads, tq), F32),
                        pltpu.VMEM((aw, tq), F32), pltpu.VMEM((tk, tq), F32),
                        pltpu.VMEM((n_heads, tk, tq), F32)],
    )
    return pl.pallas_call(
        functools.partial(_attn_body, tq=tq, tk=tk, n_heads=n_heads),
        grid_spec=grid_spec,
        out_shape=jax.ShapeDtypeStruct((b, aw, s), BF16),
        compiler_params=_cparams("masked_attn", ("parallel", "arbitrary")),
        name="masked_attn",
    )(i_tab, j_tab, q, k, vt, mask)


def _outproj_body(x_ref, *refs):
    n_in = (len(refs) - 1) // 2
    acc = x_ref[...]
    for y_ref, w_ref in zip(refs[:n_in], refs[n_in:2 * n_in]):
        acc = acc + jnp.dot(y_ref[...], w_ref[...], preferred_element_type=F32)
    refs[-1][...] = acc


def _outproj(x, ys, ws):
    n, d = x.shape
    tm = _pick_tile(n, 512, 8)
    row = lambda i: (i, 0)
    fixed = lambda i: (0, 0)
    return pl.pallas_call(
        _outproj_body,
        grid=(n // tm,),
        in_specs=[pl.BlockSpec((tm, d), row)] + [pl.BlockSpec((tm, y.shape[1]), row) for y in ys]
                 + [pl.BlockSpec(w.shape, fixed) for w in ws],
        out_specs=pl.BlockSpec((tm, d), row),
        out_shape=jax.ShapeDtypeStruct((n, d), F32),
        compiler_params=_cparams("outproj", ("parallel",)),
        name="outproj",
    )(x, *ys, *ws)


def _od_prep_body(x_ref, g_ref, w_ref, cw_ref, alog_ref, dtb_ref,
                  q_ref, k_ref, v_ref, z_ref, bg_ref, p_ref, *, tiles_per_seq, n_heads):
    tm = x_ref.shape[0]
    hist = DN_KERNEL_ROWS
    d = n_heads * DN_HEAD_DIM
    _project_rows(x_ref, g_ref, w_ref, p_ref, tiles_per_seq)
    ktaps = cw_ref.shape[0]
    for part, out_ref in enumerate((q_ref, k_ref, v_ref)):
        for h in range(n_heads):
            col = part * d + h * DN_HEAD_DIM
            block = p_ref[0:hist + tm, col:col + DN_HEAD_DIM]
            conv = None
            for j in range(ktaps):
                shift = ktaps - 1 - j
                delayed = block if shift == 0 else pltpu.roll(block, shift, 0)
                term = delayed[hist:hist + tm] * cw_ref[j:j + 1, col:col + DN_HEAD_DIM]
                conv = term if conv is None else conv + term
            u = _silu(conv)
            if part < 2:
                u = u * lax.rsqrt(jnp.sum(u * u, axis=-1, keepdims=True) + NORM_EPS)
            if part == 0:
                u = u * DN_HEAD_DIM ** -0.5
            out_ref[:, h * DN_HEAD_DIM:(h + 1) * DN_HEAD_DIM] = u
    z_ref[...] = p_ref[hist:hist + tm, 3 * d:4 * d]
    slab = p_ref[hist:hist + tm, 4 * d:4 * d + V7X_LANES]
    lane = lax.broadcasted_iota(I32, slab.shape, 1)
    xs = slab + dtb_ref[...]
    softplus = jnp.maximum(xs, 0.0) + jnp.log1p(jnp.exp(-jnp.abs(xs)))
    bg_ref[...] = jnp.where(lane < n_heads, jax.nn.sigmoid(slab), -jnp.exp(alog_ref[...]) * softplus)


def _od_prep(x, gain, w_in, conv_w, a_log, dt_bias, seq):
    n, d = x.shape
    n_heads = d // DN_HEAD_DIM
    ppad = w_in.shape[1]
    tm = _pick_tile(seq, 512, 8)
    row = lambda i: (i, 0)
    fixed = lambda i: (0, 0)
    pad = lambda t: jnp.pad(t.astype(F32), (n_heads, V7X_LANES - 2 * n_heads)).reshape(1, V7X_LANES)
    outs = [jax.ShapeDtypeStruct((n, d), F32)] * 4 + [jax.ShapeDtypeStruct((n, V7X_LANES), F32)]
    return pl.pallas_call(
        functools.partial(_od_prep_body, tiles_per_seq=seq // tm, n_heads=n_heads),
        grid=(n // tm,),
        in_specs=[pl.BlockSpec((tm, d), row), pl.BlockSpec((1, d), fixed), pl.BlockSpec((d, ppad), fixed, pipeline_mode=pl.Buffered(1)),
                  pl.BlockSpec(conv_w.shape, fixed), pl.BlockSpec((1, V7X_LANES), fixed),
                  pl.BlockSpec((1, V7X_LANES), fixed)],
        out_specs=[pl.BlockSpec((tm, d), row)] * 4 + [pl.BlockSpec((tm, V7X_LANES), row)],
        out_shape=outs,
        scratch_shapes=[pltpu.VMEM((tm + DN_KERNEL_ROWS, ppad), F32)],
        compiler_params=_cparams("od_prep", ("arbitrary",)),
        name="od_prep",
    )(x, gain.reshape(1, d), w_in, conv_w, pad(a_log), pad(dt_bias))


def _delta_body(q_ref, k_ref, v_ref, z_ref, bg_ref, bgt_ref, gain_ref, y_ref, state_ref, *, n_heads, n_chunks):
    c_len = DN_CHUNK
    hd = DN_HEAD_DIM
    heads = range(n_heads)

    @pl.when(pl.program_id(1) == 0)
    def _():
        state_ref[...] = jnp.zeros_like(state_ref)

    ri = lax.broadcasted_iota(I32, (c_len, c_len), 0)
    ci = lax.broadcasted_iota(I32, (c_len, c_len), 1)
    lower = ci <= ri
    strict = ci < ri
    eye = jnp.where(ci == ri, 1.0, 0.0)

    group = max(g for g in (1, 2, 4) if n_chunks % g == 0)

    def group_step(cg, carry):
        chunk_rows = [pl.ds(pl.multiple_of((cg * group + g) * c_len, c_len), c_len) for g in range(group)]
        bg = [bg_ref[0, chunk_rows[g], :] for g in range(group)]
        bgt = [bgt_ref[0, cg * group + g] for g in range(group)]
        units = [(g, h) for g in range(group) for h in heads]
        cols = lambda h: slice(h * hd, (h + 1) * hd)
        q = [q_ref[0, chunk_rows[g], cols(h)] for g, h in units]
        k = [k_ref[0, chunk_rows[g], cols(h)] for g, h in units]
        v = [v_ref[0, chunk_rows[g], cols(h)] for g, h in units]
        n_units = range(len(units))
        beta = [bg[g][:, h:h + 1] for g, h in units]
        gc_col = [jnp.sum(jnp.where(lower, bgt[g][n_heads + h:n_heads + h + 1, :], 0.0), axis=1, keepdims=True)
                  for g, h in units]
        gc_row = [jnp.sum(jnp.where(ri <= ci, bg[g][:, n_heads + h:n_heads + h + 1], 0.0), axis=0, keepdims=True)
                  for g, h in units]
        decay = [jnp.where(lower, jnp.exp(jnp.where(lower, gc_col[u] - gc_row[u], 0.0)), 0.0) for u in n_units]
        kb = [k[u] * beta[u] for u in n_units]
        kbf = [k[u].astype(BF16) for u in n_units]
        kq = [_dot_nt(jnp.concatenate([kb[u], q[u]], axis=0).astype(BF16), kbf[u]) for u in n_units]
        intra = [(kq[u][c_len:] * decay[u]).astype(BF16) for u in n_units]
        pw = [jnp.where(strict, -(kq[u][:c_len] * decay[u]), 0.0) for u in n_units]
        t_inv = [eye + pw[u] for u in n_units]
        span = 1
        while 2 * span < c_len:
            pwb = [p.astype(BF16) for p in pw]
            pw = [jnp.dot(pwb[u], pwb[u], preferred_element_type=F32) for u in n_units]
            t_inv = [t_inv[u] + jnp.dot(t_inv[u].astype(BF16), pw[u].astype(BF16), preferred_element_type=F32)
                     for u in n_units]
            span *= 2
        e_gc = [jnp.exp(gc_col[u]) for u in n_units]
        uw = [jnp.dot(t_inv[u].astype(BF16),
                      jnp.concatenate([v[u] * beta[u], kb[u] * e_gc[u]], axis=1).astype(BF16),
                      preferred_element_type=F32) for u in n_units]
        g_last = [gc_col[u][c_len - 1:c_len, :] for u in n_units]
        k_dec_t = [(k[u] * jnp.exp(g_last[u] - gc_col[u])).T.astype(BF16) for u in n_units]
        wq = [jnp.concatenate([uw[u][:, hd:], q[u] * e_gc[u]], axis=0).astype(BF16) for u in n_units]
        state = [state_ref[h] for h in heads]
        for g in range(group):
            us = [g * n_heads + h for h in heads]
            ws = [jnp.dot(wq[u], state[h].astype(BF16), preferred_element_type=F32) for h, u in zip(heads, us)]
            v_new = [(uw[u][:, :hd] - ws[h][:c_len]).astype(BF16) for h, u in zip(heads, us)]
            out = [ws[h][c_len:] + jnp.dot(intra[u], v_new[h], preferred_element_type=F32)
                   for h, u in zip(heads, us)]
            state = [state[h] * jnp.exp(g_last[u]) + jnp.dot(k_dec_t[u], v_new[h], preferred_element_type=F32)
                     for h, u in zip(heads, us)]
            for h in heads:
                o = out[h] * lax.rsqrt(jnp.mean(out[h] * out[h], axis=-1, keepdims=True) + NORM_EPS) * gain_ref[...]
                zh = z_ref[0, chunk_rows[g], cols(h)]
                y_ref[0, chunk_rows[g], cols(h)] = (o * _silu(zh)).astype(y_ref.dtype)
        for h in heads:
            state_ref[h] = state[h]
        return carry

    lax.fori_loop(0, n_chunks // group, group_step, 0)


def _delta_rule(q, k, v, z, bg, bgt, o_gain):
    b, s, d = q.shape
    n_heads = d // DN_HEAD_DIM
    rows = _pick_tile(s, 512, DN_CHUNK)
    n_chunks = rows // DN_CHUNK
    blk = lambda bi, i: (bi, i, 0)
    return pl.pallas_call(
        functools.partial(_delta_body, n_heads=n_heads, n_chunks=n_chunks),
        grid=(b, s // rows),
        in_specs=[pl.BlockSpec((1, rows, d), blk)] * 4 + [
            pl.BlockSpec((1, rows, V7X_LANES), blk),
            pl.BlockSpec((1, n_chunks, 2 * n_heads, DN_CHUNK), lambda bi, i: (bi, i, 0, 0)),
            pl.BlockSpec((1, DN_HEAD_DIM), lambda bi, i: (0, 0))],
        out_specs=pl.BlockSpec((1, rows, d), blk),
        out_shape=jax.ShapeDtypeStruct((b, s, d), BF16),
        scratch_shapes=[pltpu.VMEM((n_heads, DN_HEAD_DIM, DN_HEAD_DIM), F32)],
        compiler_params=_cparams("delta_rule", ("parallel", "arbitrary")),
        name="delta_rule",
    )(q, k, v, z, bg, bgt, o_gain.reshape(1, DN_HEAD_DIM))


def _rope_tables(positions):
    half = ROPE_DIM // 2
    inv_freq = ROPE_THETA ** (-jnp.arange(0, ROPE_DIM, 2, dtype=F32) / ROPE_DIM)
    ang = positions.astype(F32).reshape(-1, 1) * inv_freq
    cos, sin = jnp.cos(ang), jnp.sin(ang)
    n = ang.shape[0]
    rest = ATT_HEAD_DIM - ROPE_DIM
    zeros = lambda m: jnp.zeros((n, m), F32)
    c = jnp.concatenate([cos, cos, jnp.ones((n, rest), F32)], axis=1)
    sa = jnp.concatenate([-sin, zeros(half + rest)], axis=1)
    sb = jnp.concatenate([zeros(half), sin, zeros(rest)], axis=1)
    reps = V7X_LANES // ATT_HEAD_DIM
    return tuple(jnp.tile(t, (1, reps)) for t in (c, sa, sb))


def _pad_cols(w, mult):
    pad = (-w.shape[-1]) % mult
    return jnp.pad(w, [(0, 0)] * (w.ndim - 1) + [(0, pad)])


def kernel(x, positions, norm_gain, ffn_w_gate, ffn_w_up, ffn_w_down, ev_w_in, ev_conv_w, ev_w_out,
           od_w_in, od_conv_w, od_a_log, od_dt_bias, od_o_gain, od_w_out, final_gain):
    bsz, seq, d = x.shape
    depth = norm_gain.shape[0]
    n = bsz * seq
    assert d % (2 * V7X_LANES) == 0 and seq % V7X_LANES == 0
    top_k = min(TOPK_MAX, seq // 4)
    tk = _pick_tile(seq, 512, V7X_LANES)
    dff = ffn_w_gate.shape[-1]

    wg = ffn_w_gate.reshape(depth * 2, d, dff)
    wu = ffn_w_up.reshape(depth * 2, d, dff)
    wd = ffn_w_down.reshape(depth * 2, dff, d)
    ev_in = _pad_cols(ev_w_in, V7X_LANES).astype(BF16)
    ev_out = ev_w_out.astype(BF16)
    od_in = _pad_cols(od_w_in, V7X_LANES).astype(BF16)
    od_out = od_w_out.astype(BF16)
    tables = _rope_tables(positions)
    half_w = d // 2

    x = x.reshape(n, d)
    for layer in range(depth):
        i = layer // 2
        x = _ffn(x, norm_gain[layer, 0], wg, wu, wd, 2 * layer)
        if layer % 2 == 0:
            yc, q, k, v, qi, kiw = _ev_prep(x, norm_gain[layer, 1], ev_in[i], ev_conv_w[i], tables, seq)
            kiw3 = kiw.reshape(bsz, seq, V7X_LANES)
            ki = kiw3[:, :, :IDX_DIM].astype(BF16).reshape(bsz, seq // tk, tk, IDX_DIM)
            n_idx = half_w // IDX_DIM
            wt = jnp.swapaxes(kiw3[:, :, IDX_DIM:IDX_DIM + n_idx], 1, 2)
            to3 = lambda t: t.reshape(bsz, seq, half_w)
            mask = _topk_mask(to3(qi), wt, ki, top_k, tk)
            ya_t = _masked_attention(to3(q), to3(k), jnp.swapaxes(to3(v), 1, 2), mask, tk)
            ya = jnp.swapaxes(ya_t, 1, 2).reshape(n, half_w)
            x = _outproj(x, [yc, ya], [ev_out[i, :half_w], ev_out[i, half_w:]])
        else:
            q, k, v, z, bg = _od_prep(x, norm_gain[layer, 1], od_in[i], od_conv_w[i], od_a_log[i],
                                      od_dt_bias[i], seq)
            n_heads = d // DN_HEAD_DIM
            bg3 = bg.reshape(bsz, seq, V7X_LANES)
            bgt = jnp.swapaxes(bg3[:, :, :2 * n_heads].reshape(bsz, seq // DN_CHUNK, DN_CHUNK, 2 * n_heads),
                               2, 3)
            to3 = lambda t: t.reshape(bsz, seq, d)
            y = _delta_rule(to3(q), to3(k), to3(v), to3(z), bg3, bgt, od_o_gain[i]).reshape(n, d)
            x = _outproj(x, [y], [od_out[i]])
        x = _ffn(x, norm_gain[layer, 2], wg, wu, wd, 2 * layer + 1,
                 final_gain=final_gain if layer == depth - 1 else None)
    return x.reshape(bsz, seq, d)
```
